```python
import math
import jax
import jax.numpy as jnp
from jax import lax
import numpy as np

D_MODEL = 1024
BATCH = 32
SEQ = 256
DEPTH = 4
DEC_BATCH = 2
DEC_SEQ = 1024
PAST_LEN = 256

GRID_W = 64
N_HEADS = 8
N_KV = 2
HEAD_DIM = 64
GQA_G = N_HEADS // N_KV
ATT_W = N_HEADS * HEAD_DIM
KV_W = N_KV * HEAD_DIM
WINDOW = 128
BLOCK = 128
ROPE_BASE = 10000.0
LRU_W = 512
LRU_BLOCKS = 8
LRU_BD = LRU_W // LRU_BLOCKS
LRU_CONV = 4
LRU_C = 8.0
FNET_GROUPS = 4
FNET_W = 512
FNET_GD = FNET_W // FNET_GROUPS
HY_W = 512
HY_ORDER = 2
HY_CONV = 3
HY_BANDS = 16
HY_EMB = 1 + 2 * HY_BANDS
HY_HID = 64
HY_FILT = 2 * HY_ORDER * HY_W
HY_DECAY_MIN = math.log(100.0) / 1.5
HY_DECAY_MAX = math.log(100.0) / 0.3
EVEN_IN = ATT_W + 2 * KV_W + 2 * LRU_W
EVEN_OUT = ATT_W + LRU_W
ODD_IN = FNET_W + (HY_ORDER + 1) * HY_W
ODD_OUT = FNET_W + HY_W
D_FF = 4 * D_MODEL
N_MOD = 6
N_EVEN = (DEPTH + 1) // 2
N_ODD = DEPTH // 2
EPS = 1e-6
NEG = -1e30

kernel_name = 'hybrid_diffusion_prefix_step'


def _rmsnorm(x, g):
    xf = x.astype(jnp.float32)
    y = xf * lax.rsqrt(jnp.mean(xf * xf, axis=-1, keepdims=True) + EPS)
    return (y * g.astype(jnp.float32)).astype(x.dtype)


def _modulation(cvec, w, b):
    m = jax.nn.silu(cvec) @ w + b
    return jnp.split(m[:, None, :], N_MOD, axis=-1)


def _dwconv(x, w, b, left):
    width = w.shape[0]
    L = x.shape[1]
    xp = jnp.pad(x, ((0, 0), (left, width - 1 - left), (0, 0)))
    y = b
    for k in range(width):
        y = y + xp[:, k:k + L] * w[k]
    return y


def _axial_rope(x):
    L = x.shape[1]
    rows = L // GRID_W
    row = jnp.repeat(jnp.arange(rows), GRID_W).astype(jnp.float32)
    col = jnp.tile(jnp.arange(GRID_W), rows).astype(jnp.float32)
    n = HEAD_DIM // 4
    inv = ROPE_BASE ** (-jnp.arange(n, dtype=jnp.float32) / n)
    shape = (1, L) + (1,) * (x.ndim - 3) + (n,)
    xf = x.astype(jnp.float32)
    outs = []
    for a, pos in enumerate((row, col)):
        ang = (pos[:, None] * inv[None, :]).reshape(shape)
        cos, sin = jnp.cos(ang), jnp.sin(ang)
        seg = xf[..., a * 2 * n:(a + 1) * 2 * n]
        x1, x2 = seg[..., :n], seg[..., n:]
        outs += [x1 * cos - x2 * sin, x2 * cos + x1 * sin]
    return jnp.concatenate(outs, axis=-1).astype(x.dtype)


def _sink_attend(q, k, v, sink, mask):
    s = jnp.einsum('bqhgd,bkhd->bhgqk', q, k).astype(jnp.float32) * (HEAD_DIM ** -0.5)
    if mask is not None:
        s = jnp.where(mask, s, NEG)
    sk = jnp.broadcast_to(sink.astype(jnp.float32)[None, :, :, None, None], s.shape[:-1] + (1,))
    p = jax.nn.softmax(jnp.concatenate([s, sk], axis=-1), axis=-1)[..., :-1]
    return jnp.einsum('bhgqk,bkhd->bqhgd', p.astype(v.dtype), v)


def _ctx_attention(q, k, v, sink):
    B, L = q.shape[:2]
    nb = L // BLOCK
    qb = q.reshape(B, nb, BLOCK, N_KV, GQA_G, HEAD_DIM).swapaxes(0, 1)
    o = lax.map(lambda qi: _sink_attend(qi, k, v, sink, None), qb)
    return o.swapaxes(0, 1).reshape(B, L, N_KV, GQA_G, HEAD_DIM)


def _lat_attention(q, k, v, k_ctx, v_ctx, sink):
    B, L = q.shape[:2]
    nb = L // BLOCK
    lc = k_ctx.shape[1]
    qb = q.reshape(B, nb, BLOCK, N_KV, GQA_G, HEAD_DIM).swapaxes(0, 1)
    pad = ((0, 0), (BLOCK, BLOCK), (0, 0), (0, 0))
    kp, vp = jnp.pad(k, pad), jnp.pad(v, pad)
    k_ctx = k_ctx.astype(k.dtype)
    v_ctx = v_ctx.astype(v.dtype)
    rel = jnp.arange(3 * BLOCK)[None, :] - BLOCK - jnp.arange(BLOCK)[:, None]
    win_ok = jnp.abs(rel) <= WINDOW
    ctx_ok = jnp.ones((BLOCK, lc), dtype=bool)

    def one(args):
        qi, bi = args
        start = bi * BLOCK
        kw = lax.dynamic_slice_in_dim(kp, start, 3 * BLOCK, axis=1)
        vw = lax.dynamic_slice_in_dim(vp, start, 3 * BLOCK, axis=1)
        kpos = start - BLOCK + jnp.arange(3 * BLOCK)
        in_seq = (kpos >= 0) & (kpos < L)
        mask = jnp.concatenate([win_ok & in_seq[None, :], ctx_ok], axis=1)
        return _sink_attend(qi, jnp.concatenate([kw, k_ctx], axis=1),
                            jnp.concatenate([vw, v_ctx], axis=1), sink, mask)

    o = lax.map(one, (qb, jnp.arange(nb)))
    return o.swapaxes(0, 1).reshape(B, L, N_KV, GQA_G, HEAD_DIM)


def _combine(e1, e2):
    a1, b1 = e1
    a2, b2 = e2
    return a1 * a2, a2 * b1 + b2


def _rglru_dir(x, h0, w_r, b_r, w_i, b_i, lam, reverse):
    B, L, _ = x.shape
    xb = x.reshape(B, L, LRU_BLOCKS, LRU_BD)
    r = jax.nn.sigmoid(jnp.einsum('blhi,hij->blhj', xb, w_r.astype(jnp.float32)).reshape(B, L, LRU_W) + b_r)
    gi = jax.nn.sigmoid(jnp.einsum('blhi,hij->blhj', xb, w_i.astype(jnp.float32)).reshape(B, L, LRU_W) + b_i)
    log_a = -LRU_C * r * jax.nn.softplus(-lam.astype(jnp.float32))
    a = jnp.exp(log_a)
    u = jnp.sqrt(-jnp.expm1(2.0 * log_a)) * (gi * x)
    if reverse:
        a, u = a[:, ::-1], u[:, ::-1]
    A, S = lax.associative_scan(_combine, (a, u), axis=1)
    h = A * h0.astype(jnp.float32)[:, None, :] + S
    return h[:, ::-1] if reverse else h


def _lru_branch(xr, g, h0f, h0b, conv_w, conv_b, w_r, b_r, w_i, b_i, lam):
    xc = _dwconv(xr, conv_w, conv_b, LRU_CONV // 2).astype(jnp.float32)
    hf = _rglru_dir(xc, h0f, w_r[0], b_r[0], w_i[0], b_i[0], lam[0], False)
    hb = _rglru_dir(xc, h0b, w_r[1], b_r[1], w_i[1], b_i[1], lam[1], True)
    y = (hf + hb) * jax.nn.gelu(g.astype(jnp.float32))
    return y.astype(xr.dtype), hf[:, -1], hb[:, 0]


def _even_mixer(h, k_ctx, v_ctx, h0f, h0b, w_in, w_out, sink, conv_w, conv_b, w_r, b_r, w_i, b_i, lam):
    B, L, _ = h.shape
    p = h @ w_in
    o1, o2, o3, o4 = ATT_W, ATT_W + KV_W, ATT_W + 2 * KV_W, ATT_W + 2 * KV_W + LRU_W
    q = p[..., :o1].reshape(B, L, N_KV, GQA_G, HEAD_DIM)
    k = p[..., o1:o2].reshape(B, L, N_KV, HEAD_DIM)
    v = p[..., o2:o3].reshape(B, L, N_KV, HEAD_DIM)
    xr, g = p[..., o3:o4], p[..., o4:]
    sink = sink.reshape(N_KV, GQA_G)
    if k_ctx is None:
        att = _ctx_attention(q, k, v, sink)
        h0f = h0b = jnp.zeros((B, LRU_W), jnp.float32)
    else:
        att = _lat_attention(_axial_rope(q), _axial_rope(k), v, k_ctx, v_ctx, sink)
    y_lru, sf, sb = _lru_branch(xr, g, h0f, h0b, conv_w, conv_b, w_r, b_r, w_i, b_i, lam)
    out = jnp.concatenate([att.reshape(B, L, ATT_W), y_lru], axis=-1) @ w_out
    return out, k, v, sf, sb


def _hyena_filters(L, w1, b1, w2, b2, w3, freq, log_decay):
    t = jnp.arange(L, dtype=jnp.float32)
    tn = t / L
    bands = jnp.linspace(1e-4, HY_BANDS - 1, HY_BANDS, dtype=jnp.float32)
    w = (2.0 * math.pi / L) * t
    z = jnp.concatenate([tn[:, None], jnp.cos(w[:, None] * bands), -jnp.sin(w[:, None] * bands)], axis=-1)
    hid = jnp.sin(freq[0].astype(jnp.float32) * (z @ w1.astype(jnp.float32) + b1))
    hid = jnp.sin(freq[1].astype(jnp.float32) * (hid @ w2.astype(jnp.float32) + b2))
    filt = (hid @ w3.astype(jnp.float32)) * jnp.exp(-tn[:, None] * jnp.exp(log_decay.astype(jnp.float32)))
    filt = filt.reshape(L, 2, HY_ORDER, HY_W)
    two = jnp.concatenate([filt[:, 0], filt[::-1, 1]], axis=0)
    two = two * lax.rsqrt(jnp.sum(two * two, axis=0, keepdims=True) + EPS)
    return jnp.fft.rfft(two, axis=0)


def _fft_conv(u, kf):
    L = u.shape[1]
    U = jnp.fft.rfft(u, n=2 * L, axis=1)
    return jnp.fft.irfft(U * kf[None], n=2 * L, axis=1)[:, :L]


def _odd_mixer(h, w_in, w_out, conv_w, conv_b, w1, b1, w2, b2, w3, freq, log_decay, hy_bias):
    B, L, _ = h.shape
    p = h @ w_in
    f = p[..., :FNET_W].astype(jnp.float32).reshape(B, L, FNET_GROUPS, FNET_GD)
    yf = jnp.fft.fft2(f, axes=(1, 3), norm='ortho').real.reshape(B, L, FNET_W)
    u = _dwconv(p[..., FNET_W:], conv_w, conv_b, HY_CONV // 2).astype(jnp.float32)
    v, x1, x2 = jnp.split(u, HY_ORDER + 1, axis=-1)
    kf = _hyena_filters(L, w1, b1, w2, b2, w3, freq, log_decay)
    z = v
    for n, gate in enumerate((x1, x2)):
        z = gate * (_fft_conv(z, kf[:, n]) + hy_bias[n].astype(jnp.float32) * z)
    return jnp.concatenate([yf, z], axis=-1).astype(h.dtype) @ w_out


def _mlp(h, w1, w2):
    return jnp.square(jax.nn.relu(h @ w1)) @ w2


def setup_inputs(seed: int = 0) -> dict:
    key = jax.random.key(seed)
    ks = iter(jax.random.split(key, 40))

    def nrm(shape, s):
        return jax.random.normal(next(ks), shape, jnp.float32) * s

    lam_u = jax.random.uniform(next(ks), (N_EVEN, 2, LRU_W), jnp.float32, 0.9, 0.999)
    lam_s = lam_u ** (1.0 / LRU_C)
    decay0 = jnp.log(jnp.linspace(HY_DECAY_MIN, HY_DECAY_MAX, HY_FILT, dtype=jnp.float32))
    return {
        'x_prompt': nrm((BATCH, SEQ, D_MODEL), 1.0),
        'x_sample': nrm((DEC_BATCH, DEC_SEQ, D_MODEL), 1.0),
        'c': nrm((DEC_BATCH, D_MODEL), 1.0),
        'cache_k': nrm((DEC_BATCH, N_EVEN, PAST_LEN, N_KV, HEAD_DIM), 1.0),
        'cache_v': nrm((DEC_BATCH, N_EVEN, PAST_LEN, N_KV, HEAD_DIM), 1.0),
        'state_lru': nrm((DEC_BATCH, N_EVEN, 2, LRU_W), 0.5),
        'c_ctx': nrm((D_MODEL,), 1.0),
        'mod_w': nrm((DEPTH, D_MODEL, N_MOD * D_MODEL), 0.5 * D_MODEL ** -0.5),
        'mod_b': nrm((DEPTH, N_MOD * D_MODEL), 0.02),
        'norm_mix': 1.0 + nrm((DEPTH, D_MODEL), 0.02),
        'norm_mlp': 1.0 + nrm((DEPTH, D_MODEL), 0.02),
        'norm_final': 1.0 + nrm((D_MODEL,), 0.02),
        'mlp_w1': nrm((DEPTH, D_MODEL, D_FF), D_MODEL ** -0.5),
        'mlp_w2': nrm((DEPTH, D_FF, D_MODEL), D_FF ** -0.5),
        'ev_w_in': nrm((N_EVEN, D_MODEL, EVEN_IN), D_MODEL ** -0.5),
        'ev_w_out': nrm((N_EVEN, EVEN_OUT, D_MODEL), EVEN_OUT ** -0.5),
        'attn_sink': nrm((N_EVEN, N_HEADS), 0.5),
        'lru_conv_w': nrm((N_EVEN, LRU_CONV, LRU_W), LRU_CONV ** -0.5),
        'lru_conv_b': nrm((N_EVEN, LRU_W), 0.02),
        'lru_w_r': nrm((N_EVEN, 2, LRU_BLOCKS, LRU_BD, LRU_BD), LRU_BD ** -0.5),
        'lru_b_r': nrm((N_EVEN, 2, LRU_W), 0.02),
        'lru_w_i': nrm((N_EVEN, 2, LRU_BLOCKS, LRU_BD, LRU_BD), LRU_BD ** -0.5),
        'lru_b_i': nrm((N_EVEN, 2, LRU_W), 0.02),
        'lru_lambda': jnp.log(lam_s) - jnp.log1p(-lam_s),
        'od_w_in': nrm((N_ODD, D_MODEL, ODD_IN), D_MODEL ** -0.5),
        'od_w_out': nrm((N_ODD, ODD_OUT, D_MODEL), ODD_OUT ** -0.5),
        'hy_conv_w': nrm((N_ODD, HY_CONV, (HY_ORDER + 1) * HY_W), HY_CONV ** -0.5),
        'hy_conv_b': nrm((N_ODD, (HY_ORDER + 1) * HY_W), 0.02),
        'hy_w1': nrm((N_ODD, HY_EMB, HY_HID), HY_EMB ** -0.5),
        'hy_b1': nrm((N_ODD, HY_HID), 0.02),
        'hy_w2': nrm((N_ODD, HY_HID, HY_HID), HY_HID ** -0.5),
        'hy_b2': nrm((N_ODD, HY_HID), 0.02),
        'hy_w3': nrm((N_ODD, HY_HID, HY_FILT), HY_HID ** -0.5),
        'hy_freq': 1.0 + nrm((N_ODD, 2, HY_HID), 0.02),
        'hy_log_decay': decay0[None, :] + nrm((N_ODD, HY_FILT), 0.02),
        'hy_bias': nrm((N_ODD, HY_ORDER, HY_W), 0.1),
    }


def reference(x_prompt, x_sample, c, cache_k, cache_v, state_lru, c_ctx, mod_w, mod_b, norm_mix, norm_mlp,
              norm_final, mlp_w1, mlp_w2, ev_w_in, ev_w_out, attn_sink, lru_conv_w, lru_conv_b, lru_w_r,
              lru_b_r, lru_w_i, lru_b_i, lru_lambda, od_w_in, od_w_out, hy_conv_w, hy_conv_b, hy_w1, hy_b1,
              hy_w2, hy_b2, hy_w3, hy_freq, hy_log_decay, hy_bias):
    xp, xs = x_prompt, x_sample
    k_list, v_list, s_list = [], [], []
    for l in range(DEPTH):
        mp = _modulation(c_ctx[None, :], mod_w[l], mod_b[l])
        ms = _modulation(c, mod_w[l], mod_b[l])
        hp = _rmsnorm(xp, norm_mix[l]) * (1.0 + mp[1]) + mp[0]
        hs = _rmsnorm(xs, norm_mix[l]) * (1.0 + ms[1]) + ms[0]
        j = l // 2
        if l % 2 == 0:
            ev = (ev_w_in[j], ev_w_out[j], attn_sink[j], lru_conv_w[j], lru_conv_b[j], lru_w_r[j],
                  lru_b_r[j], lru_w_i[j], lru_b_i[j], lru_lambda[j])
            op, kc, vc, sf, sb = _even_mixer(hp, None, None, None, None, *ev)
            os_ = _even_mixer(hs, cache_k[:, j], cache_v[:, j], state_lru[:, j, 0], state_lru[:, j, 1], *ev)[0]
            k_list.append(kc)
            v_list.append(vc)
            s_list.append(jnp.stack([sf, sb], axis=1))
        else:
            od = (od_w_in[j], od_w_out[j], hy_conv_w[j], hy_conv_b[j], hy_w1[j], hy_b1[j], hy_w2[j],
                  hy_b2[j], hy_w3[j], hy_freq[j], hy_log_decay[j], hy_bias[j])
            op = _odd_mixer(hp, *od)
            os_ = _odd_mixer(hs, *od)
        xp = xp + mp[2] * op
        xs = xs + ms[2] * os_
        hp = _rmsnorm(xp, norm_mlp[l]) * (1.0 + mp[4]) + mp[3]
        hs = _rmsnorm(xs, norm_mlp[l]) * (1.0 + ms[4]) + ms[3]
        xp = xp + mp[5] * _mlp(hp, mlp_w1[l], mlp_w2[l])
        xs = xs + ms[5] * _mlp(hs, mlp_w1[l], mlp_w2[l])
    y_prompt = _rmsnorm(xp, norm_final)
    y_sample = _rmsnorm(xs, norm_final)
    k_state = jnp.stack(k_list, axis=1)
    v_state = jnp.stack(v_list, axis=1)
    lru_state = jnp.stack(s_list, axis=1).astype(x_prompt.dtype)
    return (y_prompt, y_sample, k_state, v_state, lru_state)
```

```python
import functools
import math

import numpy as np
import jax
import jax.numpy as jnp
from jax import lax
from jax.experimental import pallas as pl
from jax.experimental.pallas import tpu as pltpu

F32 = jnp.float32
BF16 = jnp.bfloat16

D_MODEL = 1024
BATCH = 32
SEQ = 256
DEPTH = 4
DEC_BATCH = 2
DEC_SEQ = 1024
PAST_LEN = 256
GRID_W = 64
N_HEADS = 8
N_KV = 2
HEAD_DIM = 64
GQA_G = N_HEADS // N_KV
ATT_W = N_HEADS * HEAD_DIM
KV_W = N_KV * HEAD_DIM
WINDOW = 128
BLOCK = 128
ROPE_BASE = 10000.0
LRU_W = 512
LRU_BLOCKS = 8
LRU_BD = LRU_W // LRU_BLOCKS
LRU_CONV = 4
LRU_C = 8.0
FNET_GROUPS = 4
FNET_W = 512
FNET_GD = FNET_W // FNET_GROUPS
HY_W = 512
HY_ORDER = 2
HY_CONV = 3
HY_BANDS = 16
HY_EMB = 1 + 2 * HY_BANDS
HY_HID = 64
HY_FILT = 2 * HY_ORDER * HY_W
EVEN_IN = ATT_W + 2 * KV_W + 2 * LRU_W
ODD_IN = FNET_W + (HY_ORDER + 1) * HY_W
D_FF = 4 * D_MODEL
N_MOD = 6
EPS = 1e-6
NEG = -1e30

N_CTX = BATCH * SEQ
N_LAT = DEC_BATCH * DEC_SEQ
N_TOK = N_CTX + N_LAT
N_GROUPS = 1 + DEC_BATCH

VMEM_LIMIT = 52 * 1024 * 1024
TM = 512
CB = 256
LCB = 128
FF_CHUNK = 1024
EMB_PAD = 128


def _params(*sem):
    return pltpu.CompilerParams(dimension_semantics=sem, vmem_limit_bytes=VMEM_LIMIT)


def _resident(shape, index_map):
    return pl.BlockSpec(shape, index_map, pipeline_mode=pl.Buffered(1))


def _dot(a, b):
    return jnp.dot(a, b, preferred_element_type=F32)


def _dot_nt(a, b):
    return lax.dot_general(a, b, (((1,), (1,)), ((), ())), preferred_element_type=F32)


def _split(x):
    hi = x.astype(BF16)
    lo = (x - hi.astype(F32)).astype(BF16)
    return hi, lo


def _dot3(a, b):
    return _dot(a[0], b[0]) + (_dot(a[1], b[0]) + _dot(a[0], b[1]))


def _np_split(x):
    hi = np.asarray(x, dtype=np.float32).astype(BF16)
    lo = (np.asarray(x, dtype=np.float64) - hi.astype(np.float64)).astype(np.float32).astype(BF16)
    return hi, lo


def _group_of_tile(i):
    return jnp.maximum(i * TM - (N_CTX - DEC_SEQ), 0) // DEC_SEQ


@functools.lru_cache(maxsize=None)
def _rdft_tables(L):
    k = np.arange(L, dtype=np.int64)[:, None]
    t = np.arange(L, dtype=np.int64)[None, :]
    ang = (k * t % (2 * L)).astype(np.float64) * (math.pi / L)
    tb = 2 * L - 1 - t
    angb = (k * tb % (2 * L)).astype(np.float64) * (math.pi / L)
    return tuple(_np_split(a) for a in (np.cos(ang), -np.sin(ang), np.cos(angb), -np.sin(angb)))


@functools.lru_cache(maxsize=None)
def _fnet_tables(L):
    k = np.arange(L, dtype=np.int64)[:, None]
    t = np.arange(L, dtype=np.int64)[None, :]
    ang = (k * t % L).astype(np.float64) * (2.0 * math.pi / L)
    scale = 1.0 / math.sqrt(L * FNET_GD)
    seq = np.concatenate([np.cos(ang), -np.sin(ang)], axis=1) * scale
    m = np.arange(FNET_GD, dtype=np.int64)[:, None]
    d = np.arange(FNET_GD, dtype=np.int64)[None, :]
    angg = (m * d % FNET_GD).astype(np.float64) * (2.0 * math.pi / FNET_GD)
    grp = np.concatenate([np.cos(angg), np.sin(angg)], axis=1)
    return _np_split(seq), _np_split(grp)


@functools.lru_cache(maxsize=None)
def _rope_tables():
    L = DEC_SEQ
    n = HEAD_DIM // 4
    row = np.repeat(np.arange(L // GRID_W), GRID_W).astype(np.float32)
    col = np.tile(np.arange(GRID_W), L // GRID_W).astype(np.float32)
    inv = (np.float32(ROPE_BASE) ** (-np.arange(n, dtype=np.float32) / np.float32(n))).astype(np.float32)
    ar = (row[:, None] * inv[None, :]).astype(np.float64)
    ac = (col[:, None] * inv[None, :]).astype(np.float64)
    cos = np.concatenate([np.cos(ar), np.cos(ar), np.cos(ac), np.cos(ac)], axis=1)
    sin = np.concatenate([-np.sin(ar), np.sin(ar), -np.sin(ac), np.sin(ac)], axis=1)
    cos = np.tile(cos, (1, 2)).astype(np.float32)
    sin = np.tile(sin, (1, 2)).astype(np.float32)
    return cos, sin


@functools.lru_cache(maxsize=None)
def _hyena_embedding(L):
    t = np.arange(L, dtype=np.float32)
    tn = (t / np.float32(L)).astype(np.float64)
    bands = np.linspace(1e-4, HY_BANDS - 1, HY_BANDS, dtype=np.float32).astype(np.float64)
    w = (np.float32(2.0 * math.pi / L) * t).astype(np.float64)
    z = np.concatenate([tn[:, None], np.cos(w[:, None] * bands), -np.sin(w[:, None] * bands)], axis=-1)
    zp = np.zeros((L, EMB_PAD), np.float64)
    zp[:, :HY_EMB] = z
    return _np_split(zp)


MOD_TN = 1536


def _mod_kernel(c_ref, w_ref, b_ref, o_ref):
    c = c_ref[...]
    s = c * jax.nn.sigmoid(c)
    o_ref[0] = _dot3(_split(s), _split(w_ref[0])) + b_ref[0]


def _modulation(cvec, mod_w, mod_b):
    out = pl.pallas_call(
        _mod_kernel,
        grid=(DEPTH, N_MOD * D_MODEL // MOD_TN),
        in_specs=[
            pl.BlockSpec((8, D_MODEL), lambda l, n: (0, 0)),
            pl.BlockSpec((1, D_MODEL, MOD_TN), lambda l, n: (l, 0, n)),
            pl.BlockSpec((1, 1, MOD_TN), lambda l, n: (l, 0, n)),
        ],
        out_specs=pl.BlockSpec((1, 8, MOD_TN), lambda l, n: (l, 0, n)),
        out_shape=jax.ShapeDtypeStruct((DEPTH, 8, N_MOD * D_MODEL), F32),
        compiler_params=_params("arbitrary", "arbitrary"),
        name="modulation",
    )(cvec, mod_w, mod_b.reshape(DEPTH, 1, N_MOD * D_MODEL))
    return out.reshape(DEPTH, 8, N_MOD, D_MODEL)[:, :N_GROUPS]


def _rms(x, g):
    return x * lax.rsqrt(jnp.mean(x * x, axis=-1, keepdims=True) + EPS) * g


def _pre_kernel(x_ref, g_ref, mod_ref, w_ref, o_ref):
    h = _rms(x_ref[...], g_ref[...]) * (1.0 + mod_ref[1:2, :]) + mod_ref[0:1, :]
    o_ref[...] = _dot(h.astype(BF16), w_ref[...])


def _pre(x, g, mod, w_in):
    n_in = w_in.shape[1]
    return pl.pallas_call(
        _pre_kernel,
        grid=(N_TOK // TM,),
        in_specs=[
            pl.BlockSpec((TM, D_MODEL), lambda i: (i, 0)),
            _resident((1, D_MODEL), lambda i: (0, 0)),
            pl.BlockSpec((None, N_MOD, D_MODEL), lambda i: (_group_of_tile(i), 0, 0)),
            _resident((D_MODEL, n_in), lambda i: (0, 0)),
        ],
        out_specs=pl.BlockSpec((TM, n_in), lambda i: (i, 0)),
        out_shape=jax.ShapeDtypeStruct((N_TOK, n_in), F32),
        compiler_params=_params("arbitrary"),
        name="pre",
    )(x, g.reshape(1, D_MODEL), mod, w_in)


def _post_kernel(x_ref, m1_ref, m2_ref, mod_ref, g_ref, gf_ref, wo_ref, w1_ref, w2_ref, o_ref, *, final):
    half = wo_ref.shape[0] // 2
    o = _dot(m1_ref[...].astype(BF16), wo_ref[:half, :]) + _dot(m2_ref[...].astype(BF16), wo_ref[half:, :])
    x1 = x_ref[...] + mod_ref[2:3, :] * o
    h = (_rms(x1, g_ref[...]) * (1.0 + mod_ref[4:5, :]) + mod_ref[3:4, :]).astype(BF16)
    acc = jnp.zeros((TM, D_MODEL), F32)
    for c in range(D_FF // FF_CHUNK):
        t = jnp.maximum(_dot(h, w1_ref[:, c * FF_CHUNK:(c + 1) * FF_CHUNK]), 0.0)
        acc = acc + _dot((t * t).astype(BF16), w2_ref[c * FF_CHUNK:(c + 1) * FF_CHUNK, :])
    x2 = x1 + mod_ref[5:6, :] * acc
    o_ref[...] = _rms(x2, gf_ref[...]) if final else x2


def _post(x, m1, m2, mod, g, g_final, w_out, w1, w2, final):
    return pl.pallas_call(
        functools.partial(_post_kernel, final=final),
        grid=(N_TOK // TM,),
        in_specs=[
            pl.BlockSpec((TM, D_MODEL), lambda i: (i, 0)),
            pl.BlockSpec((TM, m1.shape[1]), lambda i: (i, 0)),
            pl.BlockSpec((TM, m2.shape[1]), lambda i: (i, 0)),
            pl.BlockSpec((None, N_MOD, D_MODEL), lambda i: (_group_of_tile(i), 0, 0)),
            _resident((1, D_MODEL), lambda i: (0, 0)),
            _resident((1, D_MODEL), lambda i: (0, 0)),
            _resident(w_out.shape, lambda i: (0, 0)),
            _resident(w1.shape, lambda i: (0, 0)),
            _resident(w2.shape, lambda i: (0, 0)),
        ],
        out_specs=pl.BlockSpec((TM, D_MODEL), lambda i: (i, 0)),
        out_shape=jax.ShapeDtypeStruct((N_TOK, D_MODEL), F32),
        compiler_params=_params("arbitrary"),
        name="post",
    )(x, m1, m2, mod, g.reshape(1, D_MODEL), g_final.reshape(1, D_MODEL), w_out, w1, w2)


SCALE = HEAD_DIM ** -0.5


def _attn_ctx_kernel(sink_ref, q_ref, k_ref, v_ref, o_ref):
    k = k_ref[...].astype(BF16)
    v = v_ref[...].astype(BF16)
    for hd in range(N_HEADS):
        kv = hd // GQA_G
        sk = sink_ref[hd]
        qh = q_ref[:, hd * HEAD_DIM:(hd + 1) * HEAD_DIM].astype(BF16)
        s = _dot_nt(qh, k[:, kv * HEAD_DIM:(kv + 1) * HEAD_DIM]) * SCALE
        m = jnp.maximum(jnp.max(s, axis=-1, keepdims=True), sk)
        e = jnp.exp(s - m)
        den = jnp.sum(e, axis=-1, keepdims=True) + jnp.exp(sk - m)
        o = _dot(e.astype(BF16), v[:, kv * HEAD_DIM:(kv + 1) * HEAD_DIM])
        o_ref[:, hd * HEAD_DIM:(hd + 1) * HEAD_DIM] = o / den


def _attn_ctx(p, sink):
    kcol = ATT_W // KV_W
    return pl.pallas_call(
        _attn_ctx_kernel,
        grid=(BATCH,),
        in_specs=[
            pl.BlockSpec(memory_space=pltpu.SMEM),
            pl.BlockSpec((SEQ, ATT_W), lambda b: (b, 0)),
            pl.BlockSpec((SEQ, KV_W), lambda b: (b, kcol)),
            pl.BlockSpec((SEQ, KV_W), lambda b: (b, kcol + 1)),
        ],
        out_specs=pl.BlockSpec((SEQ, ATT_W), lambda b: (b, 0)),
        out_shape=jax.ShapeDtypeStruct((N_CTX, ATT_W), F32),
        compiler_params=_params("arbitrary"),
        name="attn_ctx",
    )(sink, p, p, p)


def _rope(x, cos, sin):
    lane = lax.broadcasted_iota(jnp.int32, x.shape, 1)
    swapped = jnp.where(lane % 32 < 16, pltpu.roll(x, 128 - 16, 1), pltpu.roll(x, 16, 1))
    return x * cos + swapped * sin


def _attn_lat_kernel(sink_ref, q_ref, k_ref, v_ref, ck_ref, cv_ref, cq_ref, sq_ref, cos_all_ref, sin_all_ref,
                     o_ref, kp_s, vp_s):
    i = pl.program_id(1)

    @pl.when(i == 0)
    def _():
        zeros = jnp.zeros((BLOCK, KV_W), BF16)
        kp_s[0:BLOCK, :] = zeros
        vp_s[0:BLOCK, :] = zeros
        kp_s[BLOCK + DEC_SEQ:, :] = zeros
        vp_s[BLOCK + DEC_SEQ:, :] = zeros
        kp_s[BLOCK:BLOCK + DEC_SEQ, :] = _rope(k_ref[...], cos_all_ref[...], sin_all_ref[...]).astype(BF16)
        vp_s[BLOCK:BLOCK + DEC_SEQ, :] = v_ref[...].astype(BF16)

    start = pl.multiple_of(i * BLOCK, BLOCK)
    kw = kp_s[pl.ds(start, 3 * BLOCK), :]
    vw = vp_s[pl.ds(start, 3 * BLOCK), :]
    ck = ck_ref[...].astype(BF16)
    cv = cv_ref[...].astype(BF16)
    qi = lax.broadcasted_iota(jnp.int32, (BLOCK, 3 * BLOCK), 0)
    ki = lax.broadcasted_iota(jnp.int32, (BLOCK, 3 * BLOCK), 1)
    rel = ki - BLOCK - qi
    kpos = ki + (start - BLOCK)
    ok = (jnp.abs(rel) <= WINDOW) & (kpos >= 0) & (kpos < DEC_SEQ)
    cq = cq_ref[...]
    sq = sq_ref[...]
    for pair in range(N_HEADS // 2):
        q2 = _rope(q_ref[:, pair * 128:(pair + 1) * 128], cq, sq).astype(BF16)
        for sub in range(2):
            hd = pair * 2 + sub
            kv = hd // GQA_G
            sk = sink_ref[hd]
            qh = q2[:, sub * HEAD_DIM:(sub + 1) * HEAD_DIM]
            sw = _dot_nt(qh, kw[:, kv * HEAD_DIM:(kv + 1) * HEAD_DIM]) * SCALE
            sw = jnp.where(ok, sw, NEG)
            sc = _dot_nt(qh, ck[:, kv * HEAD_DIM:(kv + 1) * HEAD_DIM]) * SCALE
            m = jnp.maximum(jnp.maximum(jnp.max(sw, axis=-1, keepdims=True),
                                        jnp.max(sc, axis=-1, keepdims=True)), sk)
            ew = jnp.exp(sw - m)
            ec = jnp.exp(sc - m)
            den = (jnp.sum(ew, axis=-1, keepdims=True) + jnp.sum(ec, axis=-1, keepdims=True)
                   + jnp.exp(sk - m))
            o = (_dot(ew.astype(BF16), vw[:, kv * HEAD_DIM:(kv + 1) * HEAD_DIM])
                 + _dot(ec.astype(BF16), cv[:, kv * HEAD_DIM:(kv + 1) * HEAD_DIM]))
            o_ref[:, hd * HEAD_DIM:(hd + 1) * HEAD_DIM] = o / den


def _attn_lat(p, sink, cache_k, cache_v):
    cos, sin = _rope_tables()
    nqb = DEC_SEQ // BLOCK
    q0 = N_CTX // BLOCK
    b0 = N_CTX // DEC_SEQ
    kcol = ATT_W // KV_W
    return pl.pallas_call(
        _attn_lat_kernel,
        grid=(DEC_BATCH, nqb),
        in_specs=[
            pl.BlockSpec(memory_space=pltpu.SMEM),
            pl.BlockSpec((BLOCK, ATT_W), lambda b, i: (q0 + b * nqb + i, 0)),
            pl.BlockSpec((DEC_SEQ, KV_W), lambda b, i: (b0 + b, kcol)),
            pl.BlockSpec((DEC_SEQ, KV_W), lambda b, i: (b0 + b, kcol + 1)),
            pl.BlockSpec((None, PAST_LEN, KV_W), lambda b, i: (b, 0, 0)),
            pl.BlockSpec((None, PAST_LEN, KV_W), lambda b, i: (b, 0, 0)),
            pl.BlockSpec((BLOCK, 128), lambda b, i: (i, 0)),
            pl.BlockSpec((BLOCK, 128), lambda b, i: (i, 0)),
            pl.BlockSpec((DEC_SEQ, 128), lambda b, i: (0, 0)),
            pl.BlockSpec((DEC_SEQ, 128), lambda b, i: (0, 0)),
        ],
        out_specs=pl.BlockSpec((BLOCK, ATT_W), lambda b, i: (b * nqb + i, 0)),
        out_shape=jax.ShapeDtypeStruct((N_LAT, ATT_W), F32),
        scratch_shapes=[pltpu.VMEM((DEC_SEQ + 2 * BLOCK, KV_W), BF16),
                        pltpu.VMEM((DEC_SEQ + 2 * BLOCK, KV_W), BF16)],
        compiler_params=_params("arbitrary", "arbitrary"),
        name="attn_lat",
    )(sink, p, p, p, cache_k, cache_v, cos, sin, cos, sin)


def _shift_rows(x, row, d):
    if d == 0:
        return x
    n = x.shape[0]
    y = pltpu.roll(x, d % n, 0)
    return jnp.where((row >= d) & (row < n + d), y, 0.0)


def _gelu_tanh(x):
    return x * (0.5 * (1.0 + jnp.tanh(math.sqrt(2.0 / math.pi) * (x + 0.044715 * (x * x * x)))))


def _lru_kernel(xr_ref, g_ref, h0_ref, cw_ref, cb_ref, wg_ref, bg_ref, lam_ref, o_ref, st_ref,
                af_s, uf_s, ab_s, ub_s, *, L, G):
    sp = jnp.maximum(-lam_ref[...], 0.0) + jnp.log1p(jnp.exp(-jnp.abs(lam_ref[...])))
    row = lax.broadcasted_iota(jnp.int32, (L, LCB), 0)
    left = LRU_CONV // 2

    def prep(b, carry):
        r0 = pl.multiple_of(b * L, L)
        x = xr_ref[pl.ds(r0, L), :]
        xc = cb_ref[...]
        for k in range(LRU_CONV):
            xc = xc + _shift_rows(x, row, left - k) * cw_ref[k:k + 1, :]
        gates = _dot(xc.astype(BF16), wg_ref[...]) + bg_ref[...]
        for d, (a_s, u_s) in enumerate(((af_s, uf_s), (ab_s, ub_s))):
            r = jax.nn.sigmoid(gates[:, (2 * d) * LCB:(2 * d + 1) * LCB])
            gi = jax.nn.sigmoid(gates[:, (2 * d + 1) * LCB:(2 * d + 2) * LCB])
            log_a = (-LRU_C * r) * sp[d:d + 1, :]
            a = jnp.exp(log_a)
            a_s[pl.ds(r0, L), :] = a
            u_s[pl.ds(r0, L), :] = jnp.sqrt(jnp.tanh(-log_a) * (1.0 + a * a)) * (gi * xc)
        return carry

    lax.fori_loop(0, G, prep, 0)

    def step(t, carry):
        hf, hb = carry
        tb = L - 1 - t
        hf = af_s[pl.ds(t, G, stride=L), :] * hf + uf_s[pl.ds(t, G, stride=L), :]
        uf_s[pl.ds(t, G, stride=L), :] = hf
        hb = ab_s[pl.ds(tb, G, stride=L), :] * hb + ub_s[pl.ds(tb, G, stride=L), :]
        ub_s[pl.ds(tb, G, stride=L), :] = hb
        return hf, hb

    hf, hb = lax.fori_loop(0, L, step, (h0_ref[0], h0_ref[1]), unroll=8)
    st_ref[0] = hf
    st_ref[1] = hb

    def fin(b, carry):
        r0 = pl.multiple_of(b * L, L)
        o_ref[pl.ds(r0, L), :] = (uf_s[pl.ds(r0, L), :] + ub_s[pl.ds(r0, L), :]) * _gelu_tanh(g_ref[pl.ds(r0, L), :])
        return carry

    lax.fori_loop(0, G, fin, 0)


def _lru(p, h0, conv_w, conv_b, wg, bg, lam, *, row0, L, B, G):
    nc = LRU_W // LCB
    xcol = (ATT_W + 2 * KV_W) // LCB
    gcol = xcol + nc
    rb0 = row0 // (G * L)
    kern = functools.partial(_lru_kernel, L=L, G=G)
    return pl.pallas_call(
        kern,
        grid=(B // G, nc),
        in_specs=[
            pl.BlockSpec((G * L, LCB), lambda i, c: (rb0 + i, xcol + c)),
            pl.BlockSpec((G * L, LCB), lambda i, c: (rb0 + i, gcol + c)),
            pl.BlockSpec((2, G, LCB), lambda i, c: (0, i, c)),
            pl.BlockSpec((LRU_CONV, LCB), lambda i, c: (0, c)),
            pl.BlockSpec((1, LCB), lambda i, c: (0, c)),
            pl.BlockSpec((None, LCB, 4 * LCB), lambda i, c: (c, 0, 0)),
            pl.BlockSpec((None, 1, 4 * LCB), lambda i, c: (c, 0, 0)),
            pl.BlockSpec((2, LCB), lambda i, c: (0, c)),
        ],
        out_specs=[
            pl.BlockSpec((G * L, LCB), lambda i, c: (i, c)),
            pl.BlockSpec((2, G, LCB), lambda i, c: (0, i, c)),
        ],
        out_shape=[jax.ShapeDtypeStruct((B * L, LRU_W), F32),
                   jax.ShapeDtypeStruct((2, B, LRU_W), F32)],
        scratch_shapes=[pltpu.VMEM((G * L, LCB), F32) for _ in range(4)],
        compiler_params=_params("arbitrary", "arbitrary"),
        name="lru",
    )(p, p, h0, conv_w, conv_b.reshape(1, LRU_W), wg, bg, lam)


def _lru_gate_weights(w_r, b_r, w_i, b_i):
    nc = LRU_W // LCB
    per = LCB // LRU_BD
    eye = jnp.eye(per, dtype=F32)

    def dense(w):
        w = w.reshape(nc, per, LRU_BD, LRU_BD)
        d = jnp.einsum('chij,hg->chigj', w, eye)
        return d.reshape(nc, LCB, LCB)

    cols = [dense(w_r[0]), dense(w_i[0]), dense(w_r[1]), dense(w_i[1])]
    wg = jnp.concatenate(cols, axis=-1).astype(BF16)
    bias = [b.reshape(nc, 1, LCB) for b in (b_r[0], b_i[0], b_r[1], b_i[1])]
    return wg, jnp.concatenate(bias, axis=-1)


def _fnet_kernel(f_ref, seq_hi_ref, seq_lo_ref, grp_hi_ref, grp_lo_ref, o_ref, hi_s, lo_s, *, L):
    grp = (grp_hi_ref[...], grp_lo_ref[...])
    for gi in range(FNET_GROUPS):
        f = _split(f_ref[:, gi * FNET_GD:(gi + 1) * FNET_GD])
        cs = _dot3(f, grp)
        for part in range(2):
            hi, lo = _split(cs[:, part * FNET_GD:(part + 1) * FNET_GD])
            hi_s[part * L:(part + 1) * L, gi * FNET_GD:(gi + 1) * FNET_GD] = hi
            lo_s[part * L:(part + 1) * L, gi * FNET_GD:(gi + 1) * FNET_GD] = lo
    o_ref[...] = _dot3((seq_hi_ref[...], seq_lo_ref[...]), (hi_s[...], lo_s[...]))


def _fnet(p, *, row0, L, B):
    seq, grp = _fnet_tables(L)
    rb0 = row0 // L
    return pl.pallas_call(
        functools.partial(_fnet_kernel, L=L),
        grid=(B,),
        in_specs=[
            pl.BlockSpec((L, FNET_W), lambda b: (rb0 + b, 0)),
            _resident((L, 2 * L), lambda b: (0, 0)),
            _resident((L, 2 * L), lambda b: (0, 0)),
            _resident((FNET_GD, 2 * FNET_GD), lambda b: (0, 0)),
            _resident((FNET_GD, 2 * FNET_GD), lambda b: (0, 0)),
        ],
        out_specs=pl.BlockSpec((L, FNET_W), lambda b: (b, 0)),
        out_shape=jax.ShapeDtypeStruct((B * L, FNET_W), F32),
        scratch_shapes=[pltpu.VMEM((2 * L, FNET_W), BF16), pltpu.VMEM((2 * L, FNET_W), BF16)],
        compiler_params=_params("arbitrary"),
        name="fnet",
    )(p, seq[0], seq[1], grp[0], grp[1])


def _alternating_sign(row):
    return (1 - 2 * (row & 1)).astype(F32)


def _filter_kernel(zh_ref, zl_ref, w1_ref, b1_ref, w2_ref, b2_ref, w3f_ref, w3b_ref, fr_ref, ldf_ref, ldb_ref,
                   fc_hi, fc_lo, fs_hi, fs_lo, fcb_hi, fcb_lo, fsb_hi, fsb_lo,
                   k1_ref, k2_ref, kn_ref, *, L):
    hid = jnp.sin(fr_ref[0:1, :] * (_dot3((zh_ref[...], zl_ref[...]), _split(w1_ref[...])) + b1_ref[...]))
    hid = jnp.sin(fr_ref[1:2, :] * (_dot3(_split(hid), _split(w2_ref[...])) + b2_ref[...]))
    hid = _split(hid)
    tn = lax.broadcasted_iota(jnp.int32, (L, CB), 0).astype(F32) * (1.0 / L)
    ff = _dot3(hid, _split(w3f_ref[...])) * jnp.exp(-tn * jnp.exp(ldf_ref[...]))
    fb = _dot3(hid, _split(w3b_ref[...])) * jnp.exp(-tn * jnp.exp(ldb_ref[...]))
    ss = jnp.sum(ff * ff, axis=0, keepdims=True) + jnp.sum(fb * fb, axis=0, keepdims=True)
    scale = lax.rsqrt(ss + EPS)
    row = lax.broadcasted_iota(jnp.int32, (L, CB), 0)
    sgn = _alternating_sign(row)
    nyq = jnp.sum(sgn * ff, axis=0, keepdims=True) - jnp.sum(sgn * fb, axis=0, keepdims=True)
    ff = _split(ff)
    fb = _split(fb)
    kr = _dot3((fc_hi[...], fc_lo[...]), ff) + _dot3((fcb_hi[...], fcb_lo[...]), fb)
    ki = _dot3((fs_hi[...], fs_lo[...]), ff) + _dot3((fsb_hi[...], fsb_lo[...]), fb)
    w = jnp.where(row == 0, 0.5 / L, 1.0 / L) * scale
    k1_ref[0] = kr * w
    k2_ref[0] = ki * w
    kn_ref[0] = nyq * (scale * (0.5 / L))


def _hyena_filters(L, w1, b1, w2, b2, w3, freq, log_decay):
    zh, zl = _hyena_embedding(L)
    tabs = [t for pair in _rdft_tables(L) for t in pair]
    nc = HY_W // CB
    w1p = jnp.zeros((EMB_PAD, HY_HID), F32).at[:HY_EMB].set(w1)
    bcol = HY_ORDER * nc
    tab_spec = _resident((L, L), lambda n, c: (0, 0))
    small = lambda shape: _resident(shape, lambda n, c: (0, 0))
    return pl.pallas_call(
        functools.partial(_filter_kernel, L=L),
        grid=(HY_ORDER, nc),
        in_specs=[
            small((L, EMB_PAD)), small((L, EMB_PAD)),
            small((EMB_PAD, HY_HID)), small((1, HY_HID)),
            small((HY_HID, HY_HID)), small((1, HY_HID)),
            pl.BlockSpec((HY_HID, CB), lambda n, c: (0, n * nc + c)),
            pl.BlockSpec((HY_HID, CB), lambda n, c: (0, bcol + n * nc + c)),
            small((2, HY_HID)),
            pl.BlockSpec((1, CB), lambda n, c: (0, n * nc + c)),
            pl.BlockSpec((1, CB), lambda n, c: (0, bcol + n * nc + c)),
        ] + [tab_spec] * 8,
        out_specs=[
            pl.BlockSpec((1, L, CB), lambda n, c: (n, 0, c)),
            pl.BlockSpec((1, L, CB), lambda n, c: (n, 0, c)),
            pl.BlockSpec((1, 1, CB), lambda n, c: (n, 0, c)),
        ],
        out_shape=[jax.ShapeDtypeStruct((HY_ORDER, L, HY_W), F32),
                   jax.ShapeDtypeStruct((HY_ORDER, L, HY_W), F32),
                   jax.ShapeDtypeStruct((HY_ORDER, 1, HY_W), F32)],
        compiler_params=_params("arbitrary", "arbitrary"),
        name="hyena_filter",
    )(zh, zl, w1p, b1.reshape(1, HY_HID), w2, b2.reshape(1, HY_HID), w3, w3, freq,
      log_decay.reshape(1, HY_FILT), log_decay.reshape(1, HY_FILT), *tabs)


def _hyena_kernel(v_ref, x1_ref, x2_ref, cwv_ref, cw1_ref, cw2_ref, cbv_ref, cb1_ref, cb2_ref, bias_ref,
                  k1_ref, k2_ref, kn_ref, fc_hi, fc_lo, fs_hi, fs_lo, o_ref, *, L):
    row = lax.broadcasted_iota(jnp.int32, (L, CB), 0)
    left = HY_CONV // 2

    def conv(x_ref, cw_ref, cb_ref):
        x = x_ref[...]
        y = cb_ref[...]
        for k in range(HY_CONV):
            y = y + _shift_rows(x, row, left - k) * cw_ref[k:k + 1, :]
        return y

    fc = (fc_hi[...], fc_lo[...])
    fs = (fs_hi[...], fs_lo[...])
    sgn = _alternating_sign(row)
    z = conv(v_ref, cwv_ref, cbv_ref)
    gates = (conv(x1_ref, cw1_ref, cb1_ref), conv(x2_ref, cw2_ref, cb2_ref))
    for n in range(HY_ORDER):
        zs = _split(z)
        a = _dot3(fc, zs)
        b = _dot3(fs, zs)
        nyq = jnp.sum(sgn * z, axis=0, keepdims=True)
        k1 = k1_ref[n]
        k2 = k2_ref[n]
        p = a * k1 - b * k2
        q = a * k2 + b * k1
        y = _dot3(fc, _split(p)) + _dot3(fs, _split(q)) + sgn * (nyq * kn_ref[n])
        z = gates[n] * (y + bias_ref[n:n + 1, :] * z)
    o_ref[...] = z


def _hyena(p, conv_w, conv_b, hy_bias, k1, k2, kn, *, row0, L, B):
    tabs = _rdft_tables(L)
    tabs = [tabs[0][0], tabs[0][1], tabs[1][0], tabs[1][1]]
    nc = HY_W // CB
    rb0 = row0 // L
    c0 = FNET_W // CB
    conv_b = conv_b.reshape(1, (HY_ORDER + 1) * HY_W)
    x_spec = lambda j: pl.BlockSpec((L, CB), lambda b, c: (rb0 + b, c0 + j * nc + c))
    cw_spec = lambda j: pl.BlockSpec((HY_CONV, CB), lambda b, c: (0, j * nc + c))
    cb_spec = lambda j: pl.BlockSpec((1, CB), lambda b, c: (0, j * nc + c))
    k_spec = pl.BlockSpec((HY_ORDER, L, CB), lambda b, c: (0, 0, c))
    return pl.pallas_call(
        functools.partial(_hyena_kernel, L=L),
        grid=(B, nc),
        in_specs=[x_spec(0), x_spec(1), x_spec(2), cw_spec(0), cw_spec(1), cw_spec(2),
                  cb_spec(0), cb_spec(1), cb_spec(2),
                  pl.BlockSpec((HY_ORDER, CB), lambda b, c: (0, c)),
                  k_spec, k_spec, pl.BlockSpec((HY_ORDER, 1, CB), lambda b, c: (0, 0, c))]
                 + [_resident((L, L), lambda b, c: (0, 0))] * 4,
        out_specs=pl.BlockSpec((L, CB), lambda b, c: (b, c)),
        out_shape=jax.ShapeDtypeStruct((B * L, HY_W), F32),
        compiler_params=_params("arbitrary", "arbitrary"),
        name="hyena",
    )(p, p, p, conv_w, conv_w, conv_w, conv_b, conv_b, conv_b, hy_bias, k1, k2, kn, *tabs)


def kernel(x_prompt, x_sample, c, cache_k, cache_v, state_lru, c_ctx, mod_w, mod_b, norm_mix, norm_mlp, norm_final, mlp_w1, mlp_w2, ev_w_in, ev_w_out, attn_sink, lru_conv_w, lru_conv_b, lru_w_r, lru_b_r, lru_w_i, lru_b_i, lru_lambda, od_w_in, od_w_out, hy_conv_w, hy_conv_b, hy_w1, hy_b1, hy_w2, hy_b2, hy_w3, hy_freq, hy_log_decay, hy_bias):
    x = jnp.concatenate([x_prompt.reshape(N_CTX, D_MODEL), x_sample.reshape(N_LAT, D_MODEL)], axis=0)
    cvec = jnp.concatenate([c_ctx[None, :], c, jnp.zeros((8 - N_GROUPS, D_MODEL), F32)], axis=0)
    mods = _modulation(cvec, mod_w, mod_b)
    zero_state = jnp.zeros((2, BATCH, LRU_W), F32)
    k_list, v_list, s_list = [], [], []
    for l in range(DEPTH):
        j = l // 2
        if l % 2 == 0:
            p = _pre(x, norm_mix[l], mods[l], ev_w_in[j].astype(BF16))
            kv0 = ATT_W
            k_list.append(p[:N_CTX, kv0:kv0 + KV_W].reshape(BATCH, SEQ, N_KV, HEAD_DIM))
            v_list.append(p[:N_CTX, kv0 + KV_W:kv0 + 2 * KV_W].reshape(BATCH, SEQ, N_KV, HEAD_DIM))
            att = jnp.concatenate([
                _attn_ctx(p, attn_sink[j]),
                _attn_lat(p, attn_sink[j], cache_k[:, j].reshape(DEC_BATCH, PAST_LEN, KV_W),
                          cache_v[:, j].reshape(DEC_BATCH, PAST_LEN, KV_W)),
            ], axis=0)
            wg, bg = _lru_gate_weights(lru_w_r[j], lru_b_r[j], lru_w_i[j], lru_b_i[j])
            lru_args = (lru_conv_w[j], lru_conv_b[j], wg, bg, lru_lambda[j])
            y_ctx, st = _lru(p, zero_state, *lru_args, row0=0, L=SEQ, B=BATCH, G=8)
            h0_lat = jnp.swapaxes(state_lru[:, j], 0, 1)
            y_lat, _ = _lru(p, h0_lat, *lru_args, row0=N_CTX, L=DEC_SEQ, B=DEC_BATCH, G=DEC_BATCH)
            s_list.append(jnp.swapaxes(st, 0, 1))
            m1, m2 = att, jnp.concatenate([y_ctx, y_lat], axis=0)
            w_out = ev_w_out[j]
        else:
            p = _pre(x, norm_mix[l], mods[l], od_w_in[j].astype(BF16))
            filt = (hy_w1[j], hy_b1[j], hy_w2[j], hy_b2[j], hy_w3[j], hy_freq[j], hy_log_decay[j])
            m1 = jnp.concatenate([_fnet(p, row0=0, L=SEQ, B=BATCH),
                                  _fnet(p, row0=N_CTX, L=DEC_SEQ, B=DEC_BATCH)], axis=0)
            hy = []
            for row0, L, B in ((0, SEQ, BATCH), (N_CTX, DEC_SEQ, DEC_BATCH)):
                k1, k2, kn = _hyena_filters(L, *filt)
                hy.append(_hyena(p, hy_conv_w[j], hy_conv_b[j], hy_bias[j], k1, k2, kn, row0=row0, L=L, B=B))
            m2 = jnp.concatenate(hy, axis=0)
            w_out = od_w_out[j]
        x = _post(x, m1, m2, mods[l], norm_mlp[l], norm_final, w_out.astype(BF16),
                  mlp_w1[l].astype(BF16), mlp_w2[l].astype(BF16), final=(l == DEPTH - 1))
    y_prompt = x[:N_CTX].reshape(BATCH, SEQ, D_MODEL)
    y_sample = x[N_CTX:].reshape(DEC_BATCH, DEC_SEQ, D_MODEL)
    k_state = jnp.stack(k_list, axis=1)
    v_state = jnp.stack(v_list, axis=1)
    lru_state = jnp.stack(s_list, axis=1).astype(x_prompt.dtype)
    return (y_prompt, y_sample, k_state, v_state, lru_state)
```

```python
import functools
import math

import numpy as np
import jax
import jax.numpy as jnp
from jax import lax
from jax.experimental import pallas as pl
from jax.experimental.pallas import tpu as pltpu

F32 = jnp.float32
BF16 = jnp.bfloat16

D_MODEL = 1024
BATCH = 32
SEQ = 256
DEPTH = 4
DEC_BATCH = 2
DEC_SEQ = 1024
PAST_LEN = 256
GRID_W = 64
N_HEADS = 8
N_KV = 2
HEAD_DIM = 64
GQA_G = N_HEADS // N_KV
ATT_W = N_HEADS * HEAD_DIM
KV_W = N_KV * HEAD_DIM
WINDOW = 128
BLOCK = 128
ROPE_BASE = 10000.0
LRU_W = 512
LRU_BLOCKS = 8
LRU_BD = LRU_W // LRU_BLOCKS
LRU_CONV = 4
LRU_C = 8.0
FNET_GROUPS = 4
FNET_W = 512
FNET_GD = FNET_W // FNET_GROUPS
HY_W = 512
HY_ORDER = 2
HY_CONV = 3
HY_BANDS = 16
HY_EMB = 1 + 2 * HY_BANDS
HY_HID = 64
HY_FILT = 2 * HY_ORDER * HY_W
EVEN_IN = ATT_W + 2 * KV_W + 2 * LRU_W
ODD_IN = FNET_W + (HY_ORDER + 1) * HY_W
D_FF = 4 * D_MODEL
N_MOD = 6
EPS = 1e-6
NEG = -1e30

N_CTX = BATCH * SEQ
N_LAT = DEC_BATCH * DEC_SEQ
N_TOK = N_CTX + N_LAT
N_GROUPS = 1 + DEC_BATCH

VMEM_LIMIT = 52 * 1024 * 1024
TM = 512
CB = 256
LCB = 128
FF_CHUNK = 1024
EMB_PAD = 128


def _params(*sem):
    return pltpu.CompilerParams(dimension_semantics=sem, vmem_limit_bytes=VMEM_LIMIT)


def _resident(shape, index_map):
    return pl.BlockSpec(shape, index_map, pipeline_mode=pl.Buffered(1))


def _dot(a, b):
    return jnp.dot(a, b, preferred_element_type=F32)


def _dot_nt(a, b):
    return lax.dot_general(a, b, (((1,), (1,)), ((), ())), preferred_element_type=F32)


def _split(x):
    hi = x.astype(BF16)
    lo = (x - hi.astype(F32)).astype(BF16)
    return hi, lo


def _dot3(a, b):
    return _dot(a[0], b[0]) + (_dot(a[1], b[0]) + _dot(a[0], b[1]))


def _np_split(x):
    hi = np.asarray(x, dtype=np.float32).astype(BF16)
    lo = (np.asarray(x, dtype=np.float64) - hi.astype(np.float64)).astype(np.float32).astype(BF16)
    return hi, lo


def _group_of_tile(i):
    return jnp.maximum(i * TM - (N_CTX - DEC_SEQ), 0) // DEC_SEQ


@functools.lru_cache(maxsize=None)
def _rdft_tables(L):
    k = np.arange(L, dtype=np.int64)[:, None]
    t = np.arange(L, dtype=np.int64)[None, :]
    ang = (k * t % (2 * L)).astype(np.float64) * (math.pi / L)
    tb = 2 * L - 1 - t
    angb = (k * tb % (2 * L)).astype(np.float64) * (math.pi / L)
    return tuple(_np_split(a) for a in (np.cos(ang), -np.sin(ang), np.cos(angb), -np.sin(angb)))


@functools.lru_cache(maxsize=None)
def _fnet_tables(L):
    k = np.arange(L, dtype=np.int64)[:, None]
    t = np.arange(L, dtype=np.int64)[None, :]
    ang = (k * t % L).astype(np.float64) * (2.0 * math.pi / L)
    scale = 1.0 / math.sqrt(L * FNET_GD)
    seq = np.concatenate([np.cos(ang), -np.sin(ang)], axis=1) * scale
    m = np.arange(FNET_GD, dtype=np.int64)[:, None]
    d = np.arange(FNET_GD, dtype=np.int64)[None, :]
    angg = (m * d % FNET_GD).astype(np.float64) * (2.0 * math.pi / FNET_GD)
    grp = np.concatenate([np.cos(angg), np.sin(angg)], axis=1)
    return _np_split(seq), _np_split(grp)


@functools.lru_cache(maxsize=None)
def _rope_tables():
    L = DEC_SEQ
    n = HEAD_DIM // 4
    row = np.repeat(np.arange(L // GRID_W), GRID_W).astype(np.float32)
    col = np.tile(np.arange(GRID_W), L // GRID_W).astype(np.float32)
    inv = (np.float32(ROPE_BASE) ** (-np.arange(n, dtype=np.float32) / np.float32(n))).astype(np.float32)
    ar = (row[:, None] * inv[None, :]).astype(np.float64)
    ac = (col[:, None] * inv[None, :]).astype(np.float64)
    cos = np.concatenate([np.cos(ar), np.cos(ar), np.cos(ac), np.cos(ac)], axis=1)
    sin = np.concatenate([-np.sin(ar), np.sin(ar), -np.sin(ac), np.sin(ac)], axis=1)
    cos = np.tile(cos, (1, 2)).astype(np.float32)
    sin = np.tile(sin, (1, 2)).astype(np.float32)
    return cos, sin


@functools.lru_cache(maxsize=None)
def _hyena_embedding(L):
    t = np.arange(L, dtype=np.float32)
    tn = (t / np.float32(L)).astype(np.float64)
    bands = np.linspace(1e-4, HY_BANDS - 1, HY_BANDS, dtype=np.float32).astype(np.float64)
    w = (np.float32(2.0 * math.pi / L) * t).astype(np.float64)
    z = np.concatenate([tn[:, None], np.cos(w[:, None] * bands), -np.sin(w[:, None] * bands)], axis=-1)
    zp = np.zeros((L, EMB_PAD), np.float64)
    zp[:, :HY_EMB] = z
    return _np_split(zp)


MOD_TN = 1536


def _mod_kernel(c_ref, w_ref, b_ref, o_ref):
    c = c_ref[...]
    s = c * jax.nn.sigmoid(c)
    o_ref[0] = _dot3(_split(s), _split(w_ref[0])) + b_ref[0]


def _modulation(cvec, mod_w, mod_b):
    out = pl.pallas_call(
        _mod_kernel,
        grid=(DEPTH, N_MOD * D_MODEL // MOD_TN),
        in_specs=[
            pl.BlockSpec((8, D_MODEL), lambda l, n: (0, 0)),
            pl.BlockSpec((1, D_MODEL, MOD_TN), lambda l, n: (l, 0, n)),
            pl.BlockSpec((1, 1, MOD_TN), lambda l, n: (l, 0, n)),
        ],
        out_specs=pl.BlockSpec((1, 8, MOD_TN), lambda l, n: (l, 0, n)),
        out_shape=jax.ShapeDtypeStruct((DEPTH, 8, N_MOD * D_MODEL), F32),
        compiler_params=_params("arbitrary", "arbitrary"),
        name="modulation",
    )(cvec, mod_w, mod_b.reshape(DEPTH, 1, N_MOD * D_MODEL))
    return out.reshape(DEPTH, 8, N_MOD, D_MODEL)[:, :N_GROUPS]


def _rms(x, g):
    return x * lax.rsqrt(jnp.mean(x * x, axis=-1, keepdims=True) + EPS) * g


N_CTX_TILES = N_CTX // TM


def _row_stream(x, tile0):
    if not isinstance(x, tuple):
        return [x], [pl.BlockSpec((TM, x.shape[1]), lambda i: (i + tile0, 0))]
    ctx, lat = x
    return [ctx, lat], [
        pl.BlockSpec((TM, ctx.shape[1]), lambda i: (jnp.minimum(i + tile0, N_CTX_TILES - 1), 0)),
        pl.BlockSpec((TM, lat.shape[1]), lambda i: (jnp.maximum(i + tile0 - N_CTX_TILES, 0), 0)),
    ]


def _read_rows(refs, paired, tile):
    vals = []
    k = 0
    for p in paired:
        if p:
            vals.append(jnp.where(tile < N_CTX_TILES, refs[k][...], refs[k + 1][...]))
            k += 2
        else:
            vals.append(refs[k][...])
            k += 1
    return vals, refs[k:]


def _pre_kernel(*refs, paired, tile0):
    (x,), (g_ref, mod_ref, w_ref, o_ref) = _read_rows(refs, paired, pl.program_id(0) + tile0)
    h = _rms(x, g_ref[...]) * (1.0 + mod_ref[1:2, :]) + mod_ref[0:1, :]
    o_ref[...] = _dot(h.astype(BF16), w_ref[...])


def _pre(x, g, mod, w_in):
    n_in = w_in.shape[1]
    ops, specs = _row_stream(x, 0)
    return pl.pallas_call(
        functools.partial(_pre_kernel, paired=(isinstance(x, tuple),), tile0=0),
        grid=(N_TOK // TM,),
        in_specs=specs + [
            _resident((1, D_MODEL), lambda i: (0, 0)),
            pl.BlockSpec((None, N_MOD, D_MODEL), lambda i: (_group_of_tile(i), 0, 0)),
            _resident((D_MODEL, n_in), lambda i: (0, 0)),
        ],
        out_specs=pl.BlockSpec((TM, n_in), lambda i: (i, 0)),
        out_shape=jax.ShapeDtypeStruct((N_TOK, n_in), F32),
        compiler_params=_params("arbitrary"),
        name="pre",
    )(*ops, g.reshape(1, D_MODEL), mod, w_in)


def _post_kernel(*refs, paired, tile0, final):
    (x, m1, m2), rest = _read_rows(refs, paired, pl.program_id(0) + tile0)
    mod_ref, g_ref, gf_ref, wo_ref, w1_ref, w2_ref, o_ref = rest
    half = wo_ref.shape[0] // 2
    o = _dot(m1.astype(BF16), wo_ref[:half, :]) + _dot(m2.astype(BF16), wo_ref[half:, :])
    x1 = x + mod_ref[2:3, :] * o
    h = (_rms(x1, g_ref[...]) * (1.0 + mod_ref[4:5, :]) + mod_ref[3:4, :]).astype(BF16)
    acc = jnp.zeros((TM, D_MODEL), F32)
    for c in range(D_FF // FF_CHUNK):
        t = jnp.maximum(_dot(h, w1_ref[:, c * FF_CHUNK:(c + 1) * FF_CHUNK]), 0.0)
        acc = acc + _dot((t * t).astype(BF16), w2_ref[c * FF_CHUNK:(c + 1) * FF_CHUNK, :])
    x2 = x1 + mod_ref[5:6, :] * acc
    o_ref[...] = _rms(x2, gf_ref[...]) if final else x2


def _post(x, m1, m2, mod, g, g_final, w_out, w1, w2, *, final, tile0=0, n_tiles=N_TOK // TM):
    ops, specs, paired = [], [], []
    for s in (x, m1, m2):
        o, sp = _row_stream(s, tile0)
        ops += o
        specs += sp
        paired.append(isinstance(s, tuple))
    return pl.pallas_call(
        functools.partial(_post_kernel, paired=tuple(paired), tile0=tile0, final=final),
        grid=(n_tiles,),
        in_specs=specs + [
            pl.BlockSpec((None, N_MOD, D_MODEL), lambda i: (_group_of_tile(i + tile0), 0, 0)),
            _resident((1, D_MODEL), lambda i: (0, 0)),
            _resident((1, D_MODEL), lambda i: (0, 0)),
            _resident(w_out.shape, lambda i: (0, 0)),
            _resident(w1.shape, lambda i: (0, 0)),
            _resident(w2.shape, lambda i: (0, 0)),
        ],
        out_specs=pl.BlockSpec((TM, D_MODEL), lambda i: (i, 0)),
        out_shape=jax.ShapeDtypeStruct((n_tiles * TM, D_MODEL), F32),
        compiler_params=_params("arbitrary"),
        name="post",
    )(*ops, mod, g.reshape(1, D_MODEL), g_final.reshape(1, D_MODEL), w_out, w1, w2)


SCALE = HEAD_DIM ** -0.5


def _attn_ctx_kernel(sink_ref, q_ref, k_ref, v_ref, o_ref, ko_ref, vo_ref):
    ko_ref[...] = k_ref[...]
    vo_ref[...] = v_ref[...]
    k = k_ref[...].astype(BF16)
    v = v_ref[...].astype(BF16)
    for hd in range(N_HEADS):
        kv = hd // GQA_G
        sk = sink_ref[hd]
        qh = q_ref[:, hd * HEAD_DIM:(hd + 1) * HEAD_DIM].astype(BF16)
        s = _dot_nt(qh, k[:, kv * HEAD_DIM:(kv + 1) * HEAD_DIM]) * SCALE
        m = jnp.maximum(jnp.max(s, axis=-1, keepdims=True), sk)
        e = jnp.exp(s - m)
        den = jnp.sum(e, axis=-1, keepdims=True) + jnp.exp(sk - m)
        o = _dot(e.astype(BF16), v[:, kv * HEAD_DIM:(kv + 1) * HEAD_DIM])
        o_ref[:, hd * HEAD_DIM:(hd + 1) * HEAD_DIM] = o / den


def _attn_ctx(p, sink):
    kcol = ATT_W // KV_W
    return pl.pallas_call(
        _attn_ctx_kernel,
        grid=(BATCH,),
        in_specs=[
            pl.BlockSpec(memory_space=pltpu.SMEM),
            pl.BlockSpec((SEQ, ATT_W), lambda b: (b, 0)),
            pl.BlockSpec((SEQ, KV_W), lambda b: (b, kcol)),
            pl.BlockSpec((SEQ, KV_W), lambda b: (b, kcol + 1)),
        ],
        out_specs=[pl.BlockSpec((SEQ, ATT_W), lambda b: (b, 0)),
                   pl.BlockSpec((SEQ, KV_W), lambda b: (b, 0)),
                   pl.BlockSpec((SEQ, KV_W), lambda b: (b, 0))],
        out_shape=[jax.ShapeDtypeStruct((N_CTX, ATT_W), F32),
                   jax.ShapeDtypeStruct((N_CTX, KV_W), F32),
                   jax.ShapeDtypeStruct((N_CTX, KV_W), F32)],
        compiler_params=_params("arbitrary"),
        name="attn_ctx",
    )(sink, p, p, p)


def _rope(x, cos, sin):
    lane = lax.broadcasted_iota(jnp.int32, x.shape, 1)
    swapped = jnp.where(lane % 32 < 16, pltpu.roll(x, 128 - 16, 1), pltpu.roll(x, 16, 1))
    return x * cos + swapped * sin


def _attn_lat_kernel(sink_ref, q_ref, k_ref, v_ref, ck_ref, cv_ref, cq_ref, sq_ref, cos_all_ref, sin_all_ref,
                     o_ref, kp_s, vp_s):
    i = pl.program_id(1)

    @pl.when(i == 0)
    def _():
        zeros = jnp.zeros((BLOCK, KV_W), BF16)
        kp_s[0:BLOCK, :] = zeros
        vp_s[0:BLOCK, :] = zeros
        kp_s[BLOCK + DEC_SEQ:, :] = zeros
        vp_s[BLOCK + DEC_SEQ:, :] = zeros
        kp_s[BLOCK:BLOCK + DEC_SEQ, :] = _rope(k_ref[...], cos_all_ref[...], sin_all_ref[...]).astype(BF16)
        vp_s[BLOCK:BLOCK + DEC_SEQ, :] = v_ref[...].astype(BF16)

    start = pl.multiple_of(i * BLOCK, BLOCK)
    kw = kp_s[pl.ds(start, 3 * BLOCK), :]
    vw = vp_s[pl.ds(start, 3 * BLOCK), :]
    ck = ck_ref[...].astype(BF16)
    cv = cv_ref[...].astype(BF16)
    qi = lax.broadcasted_iota(jnp.int32, (BLOCK, 3 * BLOCK), 0)
    ki = lax.broadcasted_iota(jnp.int32, (BLOCK, 3 * BLOCK), 1)
    rel = ki - BLOCK - qi
    kpos = ki + (start - BLOCK)
    ok = (jnp.abs(rel) <= WINDOW) & (kpos >= 0) & (kpos < DEC_SEQ)
    cq = cq_ref[...]
    sq = sq_ref[...]
    for pair in range(N_HEADS // 2):
        q2 = _rope(q_ref[:, pair * 128:(pair + 1) * 128], cq, sq).astype(BF16)
        for sub in range(2):
            hd = pair * 2 + sub
            kv = hd // GQA_G
            sk = sink_ref[hd]
            qh = q2[:, sub * HEAD_DIM:(sub + 1) * HEAD_DIM]
            sw = _dot_nt(qh, kw[:, kv * HEAD_DIM:(kv + 1) * HEAD_DIM]) * SCALE
            sw = jnp.where(ok, sw, NEG)
            sc = _dot_nt(qh, ck[:, kv * HEAD_DIM:(kv + 1) * HEAD_DIM]) * SCALE
            m = jnp.maximum(jnp.maximum(jnp.max(sw, axis=-1, keepdims=True),
                                        jnp.max(sc, axis=-1, keepdims=True)), sk)
            ew = jnp.exp(sw - m)
            ec = jnp.exp(sc - m)
            den = (jnp.sum(ew, axis=-1, keepdims=True) + jnp.sum(ec, axis=-1, keepdims=True)
                   + jnp.exp(sk - m))
            o = (_dot(ew.astype(BF16), vw[:, kv * HEAD_DIM:(kv + 1) * HEAD_DIM])
                 + _dot(ec.astype(BF16), cv[:, kv * HEAD_DIM:(kv + 1) * HEAD_DIM]))
            o_ref[:, hd * HEAD_DIM:(hd + 1) * HEAD_DIM] = o / den


def _attn_lat(p, sink, cache_k, cache_v):
    cos, sin = _rope_tables()
    nqb = DEC_SEQ // BLOCK
    q0 = N_CTX // BLOCK
    b0 = N_CTX // DEC_SEQ
    kcol = ATT_W // KV_W
    return pl.pallas_call(
        _attn_lat_kernel,
        grid=(DEC_BATCH, nqb),
        in_specs=[
            pl.BlockSpec(memory_space=pltpu.SMEM),
            pl.BlockSpec((BLOCK, ATT_W), lambda b, i: (q0 + b * nqb + i, 0)),
            pl.BlockSpec((DEC_SEQ, KV_W), lambda b, i: (b0 + b, kcol)),
            pl.BlockSpec((DEC_SEQ, KV_W), lambda b, i: (b0 + b, kcol + 1)),
            pl.BlockSpec((None, PAST_LEN, KV_W), lambda b, i: (b, 0, 0)),
            pl.BlockSpec((None, PAST_LEN, KV_W), lambda b, i: (b, 0, 0)),
            pl.BlockSpec((BLOCK, 128), lambda b, i: (i, 0)),
            pl.BlockSpec((BLOCK, 128), lambda b, i: (i, 0)),
            pl.BlockSpec((DEC_SEQ, 128), lambda b, i: (0, 0)),
            pl.BlockSpec((DEC_SEQ, 128), lambda b, i: (0, 0)),
        ],
        out_specs=pl.BlockSpec((BLOCK, ATT_W), lambda b, i: (b * nqb + i, 0)),
        out_shape=jax.ShapeDtypeStruct((N_LAT, ATT_W), F32),
        scratch_shapes=[pltpu.VMEM((DEC_SEQ + 2 * BLOCK, KV_W), BF16),
                        pltpu.VMEM((DEC_SEQ + 2 * BLOCK, KV_W), BF16)],
        compiler_params=_params("arbitrary", "arbitrary"),
        name="attn_lat",
    )(sink, p, p, p, cache_k, cache_v, cos, sin, cos, sin)


def _shift_rows(x, row, d):
    if d == 0:
        return x
    n = x.shape[0]
    y = pltpu.roll(x, d % n, 0)
    return jnp.where((row >= d) & (row < n + d), y, 0.0)


def _sigmoid(x):
    return 0.5 * jnp.tanh(0.5 * x) + 0.5


def _gelu_tanh(x):
    return x * (0.5 * (1.0 + jnp.tanh(math.sqrt(2.0 / math.pi) * (x + 0.044715 * (x * x * x)))))


def _lru_kernel(xr_ref, g_ref, h0_ref, cw_ref, cb_ref, wg_ref, bg_ref, lam_ref, o_ref, st_ref,
                af_s, uf_s, ab_s, ub_s, *, L, G):
    sp = jnp.maximum(-lam_ref[...], 0.0) + jnp.log1p(jnp.exp(-jnp.abs(lam_ref[...])))
    row = lax.broadcasted_iota(jnp.int32, (L, LCB), 0)
    left = LRU_CONV // 2

    def prep(b, carry):
        r0 = pl.multiple_of(b * L, L)
        x = xr_ref[pl.ds(r0, L), :]
        xc = cb_ref[...]
        for k in range(LRU_CONV):
            xc = xc + _shift_rows(x, row, left - k) * cw_ref[k:k + 1, :]
        gates = _dot(xc.astype(BF16), wg_ref[...]) + bg_ref[...]
        for d, (a_s, u_s) in enumerate(((af_s, uf_s), (ab_s, ub_s))):
            r = _sigmoid(gates[:, (2 * d) * LCB:(2 * d + 1) * LCB])
            gi = _sigmoid(gates[:, (2 * d + 1) * LCB:(2 * d + 2) * LCB])
            log_a = (-LRU_C * r) * sp[d:d + 1, :]
            a = jnp.exp(log_a)
            a_s[pl.ds(r0, L), :] = a
            u_s[pl.ds(r0, L), :] = jnp.sqrt(jnp.tanh(-log_a) * (1.0 + a * a)) * (gi * xc)
        return carry

    lax.fori_loop(0, G, prep, 0)

    def step(t, carry):
        hf, hb = carry
        tb = L - 1 - t
        hf = af_s[pl.ds(t, G, stride=L), :] * hf + uf_s[pl.ds(t, G, stride=L), :]
        uf_s[pl.ds(t, G, stride=L), :] = hf
        hb = ab_s[pl.ds(tb, G, stride=L), :] * hb + ub_s[pl.ds(tb, G, stride=L), :]
        ub_s[pl.ds(tb, G, stride=L), :] = hb
        return hf, hb

    hf, hb = lax.fori_loop(0, L, step, (h0_ref[0], h0_ref[1]), unroll=8)
    st_ref[0] = hf
    st_ref[1] = hb

    def fin(b, carry):
        r0 = pl.multiple_of(b * L, L)
        o_ref[pl.ds(r0, L), :] = (uf_s[pl.ds(r0, L), :] + ub_s[pl.ds(r0, L), :]) * _gelu_tanh(g_ref[pl.ds(r0, L), :])
        return carry

    lax.fori_loop(0, G, fin, 0)


def _lru(p, h0, conv_w, conv_b, wg, bg, lam, *, row0, L, B, G):
    nc = LRU_W // LCB
    xcol = (ATT_W + 2 * KV_W) // LCB
    gcol = xcol + nc
    rb0 = row0 // (G * L)
    kern = functools.partial(_lru_kernel, L=L, G=G)
    return pl.pallas_call(
        kern,
        grid=(B // G, nc),
        in_specs=[
            pl.BlockSpec((G * L, LCB), lambda i, c: (rb0 + i, xcol + c)),
            pl.BlockSpec((G * L, LCB), lambda i, c: (rb0 + i, gcol + c)),
            pl.BlockSpec((2, G, LCB), lambda i, c: (0, i, c)),
            pl.BlockSpec((LRU_CONV, LCB), lambda i, c: (0, c)),
            pl.BlockSpec((1, LCB), lambda i, c: (0, c)),
            pl.BlockSpec((None, LCB, 4 * LCB), lambda i, c: (c, 0, 0)),
            pl.BlockSpec((None, 1, 4 * LCB), lambda i, c: (c, 0, 0)),
            pl.BlockSpec((2, LCB), lambda i, c: (0, c)),
        ],
        out_specs=[
            pl.BlockSpec((G * L, LCB), lambda i, c: (i, c)),
            pl.BlockSpec((2, G, LCB), lambda i, c: (0, i, c)),
        ],
        out_shape=[jax.ShapeDtypeStruct((B * L, LRU_W), F32),
                   jax.ShapeDtypeStruct((2, B, LRU_W), F32)],
        scratch_shapes=[pltpu.VMEM((G * L, LCB), F32) for _ in range(4)],
        compiler_params=_params("arbitrary", "arbitrary"),
        name="lru",
    )(p, p, h0, conv_w, conv_b.reshape(1, LRU_W), wg, bg, lam)


def _lru_gate_weights(w_r, b_r, w_i, b_i):
    nc = LRU_W // LCB
    per = LCB // LRU_BD
    eye = jnp.eye(per, dtype=F32)

    def dense(w):
        w = w.reshape(nc, per, LRU_BD, LRU_BD)
        d = jnp.einsum('chij,hg->chigj', w, eye)
        return d.reshape(nc, LCB, LCB)

    cols = [dense(w_r[0]), dense(w_i[0]), dense(w_r[1]), dense(w_i[1])]
    wg = jnp.concatenate(cols, axis=-1).astype(BF16)
    bias = [b.reshape(nc, 1, LCB) for b in (b_r[0], b_i[0], b_r[1], b_i[1])]
    return wg, jnp.concatenate(bias, axis=-1)


def _fnet_kernel(f_ref, seq_hi_ref, seq_lo_ref, grp_hi_ref, grp_lo_ref, o_ref, hi_s, lo_s, *, L):
    grp = (grp_hi_ref[...], grp_lo_ref[...])
    for gi in range(FNET_GROUPS):
        f = _split(f_ref[:, gi * FNET_GD:(gi + 1) * FNET_GD])
        cs = _dot3(f, grp)
        for part in range(2):
            hi, lo = _split(cs[:, part * FNET_GD:(part + 1) * FNET_GD])
            hi_s[part * L:(part + 1) * L, gi * FNET_GD:(gi + 1) * FNET_GD] = hi
            lo_s[part * L:(part + 1) * L, gi * FNET_GD:(gi + 1) * FNET_GD] = lo
    o_ref[...] = _dot3((seq_hi_ref[...], seq_lo_ref[...]), (hi_s[...], lo_s[...]))


def _fnet(p, *, row0, L, B):
    seq, grp = _fnet_tables(L)
    rb0 = row0 // L
    return pl.pallas_call(
        functools.partial(_fnet_kernel, L=L),
        grid=(B,),
        in_specs=[
            pl.BlockSpec((L, FNET_W), lambda b: (rb0 + b, 0)),
            _resident((L, 2 * L), lambda b: (0, 0)),
            _resident((L, 2 * L), lambda b: (0, 0)),
            _resident((FNET_GD, 2 * FNET_GD), lambda b: (0, 0)),
            _resident((FNET_GD, 2 * FNET_GD), lambda b: (0, 0)),
        ],
        out_specs=pl.BlockSpec((L, FNET_W), lambda b: (b, 0)),
        out_shape=jax.ShapeDtypeStruct((B * L, FNET_W), F32),
        scratch_shapes=[pltpu.VMEM((2 * L, FNET_W), BF16), pltpu.VMEM((2 * L, FNET_W), BF16)],
        compiler_params=_params("arbitrary"),
        name="fnet",
    )(p, seq[0], seq[1], grp[0], grp[1])


def _alternating_sign(row):
    return (1 - 2 * (row & 1)).astype(F32)


def _filter_kernel(zh_ref, zl_ref, w1_ref, b1_ref, w2_ref, b2_ref, w3f_ref, w3b_ref, fr_ref, ldf_ref, ldb_ref,
                   fc_hi, fc_lo, fs_hi, fs_lo, fcb_hi, fcb_lo, fsb_hi, fsb_lo,
                   k1_ref, k2_ref, kn_ref, *, L):
    hid = jnp.sin(fr_ref[0:1, :] * (_dot3((zh_ref[...], zl_ref[...]), _split(w1_ref[...])) + b1_ref[...]))
    hid = jnp.sin(fr_ref[1:2, :] * (_dot3(_split(hid), _split(w2_ref[...])) + b2_ref[...]))
    hid = _split(hid)
    tn = lax.broadcasted_iota(jnp.int32, (L, CB), 0).astype(F32) * (1.0 / L)
    ff = _dot3(hid, _split(w3f_ref[...])) * jnp.exp(-tn * jnp.exp(ldf_ref[...]))
    fb = _dot3(hid, _split(w3b_ref[...])) * jnp.exp(-tn * jnp.exp(ldb_ref[...]))
    ss = jnp.sum(ff * ff, axis=0, keepdims=True) + jnp.sum(fb * fb, axis=0, keepdims=True)
    scale = lax.rsqrt(ss + EPS)
    row = lax.broadcasted_iota(jnp.int32, (L, CB), 0)
    sgn = _alternating_sign(row)
    nyq = jnp.sum(sgn * ff, axis=0, keepdims=True) - jnp.sum(sgn * fb, axis=0, keepdims=True)
    ff = _split(ff)
    fb = _split(fb)
    kr = _dot3((fc_hi[...], fc_lo[...]), ff) + _dot3((fcb_hi[...], fcb_lo[...]), fb)
    ki = _dot3((fs_hi[...], fs_lo[...]), ff) + _dot3((fsb_hi[...], fsb_lo[...]), fb)
    w = jnp.where(row == 0, 0.5 / L, 1.0 / L) * scale
    k1_ref[0] = kr * w
    k2_ref[0] = ki * w
    kn_ref[0] = nyq * (scale * (0.5 / L))


def _hyena_filters(L, w1, b1, w2, b2, w3, freq, log_decay):
    zh, zl = _hyena_embedding(L)
    tabs = [t for pair in _rdft_tables(L) for t in pair]
    nc = HY_W // CB
    w1p = jnp.zeros((EMB_PAD, HY_HID), F32).at[:HY_EMB].set(w1)
    bcol = HY_ORDER * nc
    tab_spec = _resident((L, L), lambda n, c: (0, 0))
    small = lambda shape: _resident(shape, lambda n, c: (0, 0))
    return pl.pallas_call(
        functools.partial(_filter_kernel, L=L),
        grid=(HY_ORDER, nc),
        in_specs=[
            small((L, EMB_PAD)), small((L, EMB_PAD)),
            small((EMB_PAD, HY_HID)), small((1, HY_HID)),
            small((HY_HID, HY_HID)), small((1, HY_HID)),
            pl.BlockSpec((HY_HID, CB), lambda n, c: (0, n * nc + c)),
            pl.BlockSpec((HY_HID, CB), lambda n, c: (0, bcol + n * nc + c)),
            small((2, HY_HID)),
            pl.BlockSpec((1, CB), lambda n, c: (0, n * nc + c)),
            pl.BlockSpec((1, CB), lambda n, c: (0, bcol + n * nc + c)),
        ] + [tab_spec] * 8,
        out_specs=[
            pl.BlockSpec((1, L, CB), lambda n, c: (n, 0, c)),
            pl.BlockSpec((1, L, CB), lambda n, c: (n, 0, c)),
            pl.BlockSpec((1, 1, CB), lambda n, c: (n, 0, c)),
        ],
        out_shape=[jax.ShapeDtypeStruct((HY_ORDER, L, HY_W), F32),
                   jax.ShapeDtypeStruct((HY_ORDER, L, HY_W), F32),
                   jax.ShapeDtypeStruct((HY_ORDER, 1, HY_W), F32)],
        compiler_params=_params("arbitrary", "arbitrary"),
        name="hyena_filter",
    )(zh, zl, w1p, b1.reshape(1, HY_HID), w2, b2.reshape(1, HY_HID), w3, w3, freq,
      log_decay.reshape(1, HY_FILT), log_decay.reshape(1, HY_FILT), *tabs)


def _hyena_kernel(v_ref, x1_ref, x2_ref, cwv_ref, cw1_ref, cw2_ref, cbv_ref, cb1_ref, cb2_ref, bias_ref,
                  k1_ref, k2_ref, kn_ref, fc_hi, fc_lo, fs_hi, fs_lo, o_ref, *, L):
    row = lax.broadcasted_iota(jnp.int32, (L, CB), 0)
    left = HY_CONV // 2

    def conv(x_ref, cw_ref, cb_ref):
        x = x_ref[...]
        y = cb_ref[...]
        for k in range(HY_CONV):
            y = y + _shift_rows(x, row, left - k) * cw_ref[k:k + 1, :]
        return y

    fc = (fc_hi[...], fc_lo[...])
    fs = (fs_hi[...], fs_lo[...])
    sgn = _alternating_sign(row)
    z = conv(v_ref, cwv_ref, cbv_ref)
    gates = (conv(x1_ref, cw1_ref, cb1_ref), conv(x2_ref, cw2_ref, cb2_ref))
    for n in range(HY_ORDER):
        zs = _split(z)
        a = _dot3(fc, zs)
        b = _dot3(fs, zs)
        nyq = jnp.sum(sgn * z, axis=0, keepdims=True)
        k1 = k1_ref[n]
        k2 = k2_ref[n]
        p = a * k1 - b * k2
        q = a * k2 + b * k1
        y = _dot3(fc, _split(p)) + _dot3(fs, _split(q)) + sgn * (nyq * kn_ref[n])
        z = gates[n] * (y + bias_ref[n:n + 1, :] * z)
    o_ref[...] = z


def _hyena(p, conv_w, conv_b, hy_bias, k1, k2, kn, *, row0, L, B):
    tabs = _rdft_tables(L)
    tabs = [tabs[0][0], tabs[0][1], tabs[1][0], tabs[1][1]]
    nc = HY_W // CB
    rb0 = row0 // L
    c0 = FNET_W // CB
    conv_b = conv_b.reshape(1, (HY_ORDER + 1) * HY_W)
    x_spec = lambda j: pl.BlockSpec((L, CB), lambda b, c: (rb0 + b, c0 + j * nc + c))
    cw_spec = lambda j: pl.BlockSpec((HY_CONV, CB), lambda b, c: (0, j * nc + c))
    cb_spec = lambda j: pl.BlockSpec((1, CB), lambda b, c: (0, j * nc + c))
    k_spec = pl.BlockSpec((HY_ORDER, L, CB), lambda b, c: (0, 0, c))
    return pl.pallas_call(
        functools.partial(_hyena_kernel, L=L),
        grid=(B, nc),
        in_specs=[x_spec(0), x_spec(1), x_spec(2), cw_spec(0), cw_spec(1), cw_spec(2),
                  cb_spec(0), cb_spec(1), cb_spec(2),
                  pl.BlockSpec((HY_ORDER, CB), lambda b, c: (0, c)),
                  k_spec, k_spec, pl.BlockSpec((HY_ORDER, 1, CB), lambda b, c: (0, 0, c))]
                 + [_resident((L, L), lambda b, c: (0, 0))] * 4,
        out_specs=pl.BlockSpec((L, CB), lambda b, c: (b, c)),
        out_shape=jax.ShapeDtypeStruct((B * L, HY_W), F32),
        compiler_params=_params("arbitrary", "arbitrary"),
        name="hyena",
    )(p, p, p, conv_w, conv_w, conv_w, conv_b, conv_b, conv_b, hy_bias, k1, k2, kn, *tabs)


def kernel(x_prompt, x_sample, c, cache_k, cache_v, state_lru, c_ctx, mod_w, mod_b, norm_mix, norm_mlp, norm_final, mlp_w1, mlp_w2, ev_w_in, ev_w_out, attn_sink, lru_conv_w, lru_conv_b, lru_w_r, lru_b_r, lru_w_i, lru_b_i, lru_lambda, od_w_in, od_w_out, hy_conv_w, hy_conv_b, hy_w1, hy_b1, hy_w2, hy_b2, hy_w3, hy_freq, hy_log_decay, hy_bias):
    x = (x_prompt.reshape(N_CTX, D_MODEL), x_sample.reshape(N_LAT, D_MODEL))
    cvec = jnp.concatenate([c_ctx[None, :], c, jnp.zeros((8 - N_GROUPS, D_MODEL), F32)], axis=0)
    mods = _modulation(cvec, mod_w, mod_b)
    zero_state = jnp.zeros((2, BATCH, LRU_W), F32)
    k_list, v_list, s_list = [], [], []
    for l in range(DEPTH):
        j = l // 2
        if l % 2 == 0:
            p = _pre(x, norm_mix[l], mods[l], ev_w_in[j].astype(BF16))
            att_ctx, k_ctx, v_ctx = _attn_ctx(p, attn_sink[j])
            k_list.append(k_ctx.reshape(BATCH, SEQ, N_KV, HEAD_DIM))
            v_list.append(v_ctx.reshape(BATCH, SEQ, N_KV, HEAD_DIM))
            att_lat = _attn_lat(p, attn_sink[j], cache_k[:, j].reshape(DEC_BATCH, PAST_LEN, KV_W),
                                cache_v[:, j].reshape(DEC_BATCH, PAST_LEN, KV_W))
            wg, bg = _lru_gate_weights(lru_w_r[j], lru_b_r[j], lru_w_i[j], lru_b_i[j])
            lru_args = (lru_conv_w[j], lru_conv_b[j], wg, bg, lru_lambda[j])
            y_ctx, st = _lru(p, zero_state, *lru_args, row0=0, L=SEQ, B=BATCH, G=8)
            h0_lat = jnp.swapaxes(state_lru[:, j], 0, 1)
            y_lat, _ = _lru(p, h0_lat, *lru_args, row0=N_CTX, L=DEC_SEQ, B=DEC_BATCH, G=DEC_BATCH)
            s_list.append(jnp.swapaxes(st, 0, 1))
            m1, m2 = (att_ctx, att_lat), (y_ctx, y_lat)
            w_out = ev_w_out[j]
        else:
            p = _pre(x, norm_mix[l], mods[l], od_w_in[j].astype(BF16))
            filt = (hy_w1[j], hy_b1[j], hy_w2[j], hy_b2[j], hy_w3[j], hy_freq[j], hy_log_decay[j])
            m1 = (_fnet(p, row0=0, L=SEQ, B=BATCH), _fnet(p, row0=N_CTX, L=DEC_SEQ, B=DEC_BATCH))
            hy = []
            for row0, L, B in ((0, SEQ, BATCH), (N_CTX, DEC_SEQ, DEC_BATCH)):
                k1, k2, kn = _hyena_filters(L, *filt)
                hy.append(_hyena(p, hy_conv_w[j], hy_conv_b[j], hy_bias[j], k1, k2, kn, row0=row0, L=L, B=B))
            m2 = tuple(hy)
            w_out = od_w_out[j]
        post = functools.partial(_post, x, m1, m2, mods[l], norm_mlp[l], norm_final, w_out.astype(BF16),
                                 mlp_w1[l].astype(BF16), mlp_w2[l].astype(BF16))
        if l < DEPTH - 1:
            x = post(final=False)
        else:
            y_prompt = post(final=True, tile0=0, n_tiles=N_CTX_TILES).reshape(BATCH, SEQ, D_MODEL)
            y_sample = post(final=True, tile0=N_CTX_TILES, n_tiles=N_LAT // TM).reshape(DEC_BATCH, DEC_SEQ, D_MODEL)
    k_state = jnp.stack(k_list, axis=1)
    v_state = jnp.stack(v_list, axis=1)
    lru_state = jnp.stack(s_list, axis=1).astype(x_prompt.dtype)
    return (y_prompt, y_sample, k_state, v_state, lru_state)
```

```python
import functools
import math

import numpy as np
import jax
import jax.numpy as jnp
from jax import lax
from jax.experimental import pallas as pl
from jax.experimental.pallas import tpu as pltpu

F32 = jnp.float32
BF16 = jnp.bfloat16

D_MODEL = 1024
BATCH = 32
SEQ = 256
DEPTH = 4
DEC_BATCH = 2
DEC_SEQ = 1024
PAST_LEN = 256
GRID_W = 64
N_HEADS = 8
N_KV = 2
HEAD_DIM = 64
GQA_G = N_HEADS // N_KV
ATT_W = N_HEADS * HEAD_DIM
KV_W = N_KV * HEAD_DIM
WINDOW = 128
BLOCK = 128
ROPE_BASE = 10000.0
LRU_W = 512
LRU_BLOCKS = 8
LRU_BD = LRU_W // LRU_BLOCKS
LRU_CONV = 4
LRU_C = 8.0
FNET_GROUPS = 4
FNET_W = 512
FNET_GD = FNET_W // FNET_GROUPS
HY_W = 512
HY_ORDER = 2
HY_CONV = 3
HY_BANDS = 16
HY_EMB = 1 + 2 * HY_BANDS
HY_HID = 64
HY_FILT = 2 * HY_ORDER * HY_W
EVEN_IN = ATT_W + 2 * KV_W + 2 * LRU_W
ODD_IN = FNET_W + (HY_ORDER + 1) * HY_W
D_FF = 4 * D_MODEL
N_MOD = 6
EPS = 1e-6
NEG = -1e30

N_CTX = BATCH * SEQ
N_LAT = DEC_BATCH * DEC_SEQ
N_TOK = N_CTX + N_LAT
N_GROUPS = 1 + DEC_BATCH

VMEM_LIMIT = 52 * 1024 * 1024
TM = 512
CB = 256
LCB = 128
FF_CHUNK = 1024
EMB_PAD = 128


def _params(*sem):
    return pltpu.CompilerParams(dimension_semantics=sem, vmem_limit_bytes=VMEM_LIMIT)


def _resident(shape, index_map):
    return pl.BlockSpec(shape, index_map, pipeline_mode=pl.Buffered(1))


def _dot(a, b):
    return jnp.dot(a, b, preferred_element_type=F32)


def _dot_nt(a, b):
    return lax.dot_general(a, b, (((1,), (1,)), ((), ())), preferred_element_type=F32)


def _split(x):
    hi = x.astype(BF16)
    lo = (x - hi.astype(F32)).astype(BF16)
    return hi, lo


def _dot3(a, b):
    return _dot(a[0], b[0]) + (_dot(a[1], b[0]) + _dot(a[0], b[1]))


def _np_f32(x):
    return np.asarray(x, dtype=np.float32)


def _table_bf16(t):
    return jnp.asarray(t).astype(BF16)


def _np_split(x):
    hi = np.asarray(x, dtype=np.float32).astype(BF16)
    lo = (np.asarray(x, dtype=np.float64) - hi.astype(np.float64)).astype(np.float32).astype(BF16)
    return hi, lo


def _group_of_tile(i):
    return jnp.maximum(i * TM - (N_CTX - DEC_SEQ), 0) // DEC_SEQ


@functools.lru_cache(maxsize=None)
def _rdft_tables(L):
    k = np.arange(L, dtype=np.int64)[:, None]
    t = np.arange(L, dtype=np.int64)[None, :]
    ang = (k * t % (2 * L)).astype(np.float64) * (math.pi / L)
    tb = 2 * L - 1 - t
    angb = (k * tb % (2 * L)).astype(np.float64) * (math.pi / L)
    return tuple(_np_f32(a) for a in (np.cos(ang), -np.sin(ang), np.cos(angb), -np.sin(angb)))


@functools.lru_cache(maxsize=None)
def _fnet_tables(L):
    k = np.arange(L, dtype=np.int64)[:, None]
    t = np.arange(L, dtype=np.int64)[None, :]
    ang = (k * t % L).astype(np.float64) * (2.0 * math.pi / L)
    scale = 1.0 / math.sqrt(L * FNET_GD)
    seq = np.concatenate([np.cos(ang), -np.sin(ang)], axis=1) * scale
    m = np.arange(FNET_GD, dtype=np.int64)[:, None]
    d = np.arange(FNET_GD, dtype=np.int64)[None, :]
    angg = (m * d % FNET_GD).astype(np.float64) * (2.0 * math.pi / FNET_GD)
    grp = np.concatenate([np.cos(angg), np.sin(angg)], axis=1)
    return _np_f32(seq), _np_f32(grp)


@functools.lru_cache(maxsize=None)
def _rope_tables():
    L = DEC_SEQ
    n = HEAD_DIM // 4
    row = np.repeat(np.arange(L // GRID_W), GRID_W).astype(np.float32)
    col = np.tile(np.arange(GRID_W), L // GRID_W).astype(np.float32)
    inv = (np.float32(ROPE_BASE) ** (-np.arange(n, dtype=np.float32) / np.float32(n))).astype(np.float32)
    ar = (row[:, None] * inv[None, :]).astype(np.float64)
    ac = (col[:, None] * inv[None, :]).astype(np.float64)
    cos = np.concatenate([np.cos(ar), np.cos(ar), np.cos(ac), np.cos(ac)], axis=1)
    sin = np.concatenate([-np.sin(ar), np.sin(ar), -np.sin(ac), np.sin(ac)], axis=1)
    cos = np.tile(cos, (1, 2)).astype(np.float32)
    sin = np.tile(sin, (1, 2)).astype(np.float32)
    return cos, sin


@functools.lru_cache(maxsize=None)
def _hyena_embedding(L):
    t = np.arange(L, dtype=np.float32)
    tn = (t / np.float32(L)).astype(np.float64)
    bands = np.linspace(1e-4, HY_BANDS - 1, HY_BANDS, dtype=np.float32).astype(np.float64)
    w = (np.float32(2.0 * math.pi / L) * t).astype(np.float64)
    z = np.concatenate([tn[:, None], np.cos(w[:, None] * bands), -np.sin(w[:, None] * bands)], axis=-1)
    zp = np.zeros((L, EMB_PAD), np.float64)
    zp[:, :HY_EMB] = z
    return _np_split(zp)


MOD_TN = 1536


def _mod_kernel(c_ref, w_ref, b_ref, o_ref):
    c = c_ref[...]
    s = c * jax.nn.sigmoid(c)
    o_ref[0] = _dot3(_split(s), _split(w_ref[0])) + b_ref[0]


def _modulation(cvec, mod_w, mod_b):
    out = pl.pallas_call(
        _mod_kernel,
        grid=(DEPTH, N_MOD * D_MODEL // MOD_TN),
        in_specs=[
            pl.BlockSpec((8, D_MODEL), lambda l, n: (0, 0)),
            pl.BlockSpec((1, D_MODEL, MOD_TN), lambda l, n: (l, 0, n)),
            pl.BlockSpec((1, 1, MOD_TN), lambda l, n: (l, 0, n)),
        ],
        out_specs=pl.BlockSpec((1, 8, MOD_TN), lambda l, n: (l, 0, n)),
        out_shape=jax.ShapeDtypeStruct((DEPTH, 8, N_MOD * D_MODEL), F32),
        compiler_params=_params("arbitrary", "arbitrary"),
        name="modulation",
    )(cvec, mod_w, mod_b.reshape(DEPTH, 1, N_MOD * D_MODEL))
    return out.reshape(DEPTH, 8, N_MOD, D_MODEL)[:, :N_GROUPS]


def _rms(x, g):
    return x * lax.rsqrt(jnp.mean(x * x, axis=-1, keepdims=True) + EPS) * g


N_CTX_TILES = N_CTX // TM


def _row_stream(x, tile0):
    if not isinstance(x, tuple):
        return [x], [pl.BlockSpec((TM, x.shape[1]), lambda i: (i + tile0, 0))]
    ctx, lat = x
    return [ctx, lat], [
        pl.BlockSpec((TM, ctx.shape[1]), lambda i: (jnp.minimum(i + tile0, N_CTX_TILES - 1), 0)),
        pl.BlockSpec((TM, lat.shape[1]), lambda i: (jnp.maximum(i + tile0 - N_CTX_TILES, 0), 0)),
    ]


def _read_rows(refs, paired, tile):
    vals = []
    k = 0
    for p in paired:
        if p:
            vals.append(jnp.where(tile < N_CTX_TILES, refs[k][...], refs[k + 1][...]))
            k += 2
        else:
            vals.append(refs[k][...])
            k += 1
    return vals, refs[k:]


def _pre_kernel(*refs, paired, tile0):
    (x,), (g_ref, mod_ref, w_ref, o_ref) = _read_rows(refs, paired, pl.program_id(0) + tile0)
    h = _rms(x, g_ref[...]) * (1.0 + mod_ref[1:2, :]) + mod_ref[0:1, :]
    o_ref[...] = _dot(h.astype(BF16), w_ref[...])


def _pre(x, g, mod, w_in, j):
    n_in = w_in.shape[2]
    ops, specs = _row_stream(x, 0)
    return pl.pallas_call(
        functools.partial(_pre_kernel, paired=(isinstance(x, tuple),), tile0=0),
        grid=(N_TOK // TM,),
        in_specs=specs + [
            _resident((1, D_MODEL), lambda i: (0, 0)),
            pl.BlockSpec((None, N_MOD, D_MODEL), lambda i: (_group_of_tile(i), 0, 0)),
            _resident((None, D_MODEL, n_in), lambda i: (j, 0, 0)),
        ],
        out_specs=pl.BlockSpec((TM, n_in), lambda i: (i, 0)),
        out_shape=jax.ShapeDtypeStruct((N_TOK, n_in), F32),
        compiler_params=_params("arbitrary"),
        name="pre",
    )(*ops, g.reshape(1, D_MODEL), mod, w_in)


def _post_kernel(*refs, paired, tile0, final):
    (x, m1, m2), rest = _read_rows(refs, paired, pl.program_id(0) + tile0)
    mod_ref, g_ref, gf_ref, wo_ref, w1_ref, w2_ref, o_ref = rest
    half = wo_ref.shape[0] // 2
    o = _dot(m1.astype(BF16), wo_ref[:half, :]) + _dot(m2.astype(BF16), wo_ref[half:, :])
    x1 = x + mod_ref[2:3, :] * o
    h = (_rms(x1, g_ref[...]) * (1.0 + mod_ref[4:5, :]) + mod_ref[3:4, :]).astype(BF16)
    acc = jnp.zeros((TM, D_MODEL), F32)
    for c in range(D_FF // FF_CHUNK):
        t = jnp.maximum(_dot(h, w1_ref[:, c * FF_CHUNK:(c + 1) * FF_CHUNK]), 0.0)
        acc = acc + _dot((t * t).astype(BF16), w2_ref[c * FF_CHUNK:(c + 1) * FF_CHUNK, :])
    x2 = x1 + mod_ref[5:6, :] * acc
    o_ref[...] = _rms(x2, gf_ref[...]) if final else x2


def _post(x, m1, m2, mod, g, g_final, w_out, w1, w2, j, l, *, final, tile0=0, n_tiles=N_TOK // TM):
    ops, specs, paired = [], [], []
    for s in (x, m1, m2):
        o, sp = _row_stream(s, tile0)
        ops += o
        specs += sp
        paired.append(isinstance(s, tuple))
    return pl.pallas_call(
        functools.partial(_post_kernel, paired=tuple(paired), tile0=tile0, final=final),
        grid=(n_tiles,),
        in_specs=specs + [
            pl.BlockSpec((None, N_MOD, D_MODEL), lambda i: (_group_of_tile(i + tile0), 0, 0)),
            _resident((1, D_MODEL), lambda i: (0, 0)),
            _resident((1, D_MODEL), lambda i: (0, 0)),
            _resident((None,) + w_out.shape[1:], lambda i: (j, 0, 0)),
            _resident((None,) + w1.shape[1:], lambda i: (l, 0, 0)),
            _resident((None,) + w2.shape[1:], lambda i: (l, 0, 0)),
        ],
        out_specs=pl.BlockSpec((TM, D_MODEL), lambda i: (i, 0)),
        out_shape=jax.ShapeDtypeStruct((n_tiles * TM, D_MODEL), F32),
        compiler_params=_params("arbitrary"),
        name="post",
    )(*ops, mod, g.reshape(1, D_MODEL), g_final.reshape(1, D_MODEL), w_out, w1, w2)


SCALE = HEAD_DIM ** -0.5


def _attn_ctx_kernel(sink_ref, q_ref, k_ref, v_ref, o_ref, ko_ref, vo_ref):
    ko_ref[...] = k_ref[...]
    vo_ref[...] = v_ref[...]
    k = k_ref[...].astype(BF16)
    v = v_ref[...].astype(BF16)
    for hd in range(N_HEADS):
        kv = hd // GQA_G
        sk = sink_ref[hd]
        qh = q_ref[:, hd * HEAD_DIM:(hd + 1) * HEAD_DIM].astype(BF16)
        s = _dot_nt(qh, k[:, kv * HEAD_DIM:(kv + 1) * HEAD_DIM]) * SCALE
        m = jnp.maximum(jnp.max(s, axis=-1, keepdims=True), sk)
        e = jnp.exp(s - m)
        den = jnp.sum(e, axis=-1, keepdims=True) + jnp.exp(sk - m)
        o = _dot(e.astype(BF16), v[:, kv * HEAD_DIM:(kv + 1) * HEAD_DIM])
        o_ref[:, hd * HEAD_DIM:(hd + 1) * HEAD_DIM] = o / den


def _attn_ctx(p, sink):
    kcol = ATT_W // KV_W
    return pl.pallas_call(
        _attn_ctx_kernel,
        grid=(BATCH,),
        in_specs=[
            pl.BlockSpec(memory_space=pltpu.SMEM),
            pl.BlockSpec((SEQ, ATT_W), lambda b: (b, 0)),
            pl.BlockSpec((SEQ, KV_W), lambda b: (b, kcol)),
            pl.BlockSpec((SEQ, KV_W), lambda b: (b, kcol + 1)),
        ],
        out_specs=[pl.BlockSpec((SEQ, ATT_W), lambda b: (b, 0)),
                   pl.BlockSpec((SEQ, KV_W), lambda b: (b, 0)),
                   pl.BlockSpec((SEQ, KV_W), lambda b: (b, 0))],
        out_shape=[jax.ShapeDtypeStruct((N_CTX, ATT_W), F32),
                   jax.ShapeDtypeStruct((N_CTX, KV_W), F32),
                   jax.ShapeDtypeStruct((N_CTX, KV_W), F32)],
        compiler_params=_params("arbitrary"),
        name="attn_ctx",
    )(sink, p, p, p)


def _rope(x, cos, sin):
    lane = lax.broadcasted_iota(jnp.int32, x.shape, 1)
    swapped = jnp.where(lane % 32 < 16, pltpu.roll(x, 128 - 16, 1), pltpu.roll(x, 16, 1))
    return x * cos + swapped * sin


def _attn_lat_kernel(sink_ref, q_ref, k_ref, v_ref, ck_ref, cv_ref, cq_ref, sq_ref, cos_all_ref, sin_all_ref,
                     o_ref, kp_s, vp_s):
    i = pl.program_id(1)

    @pl.when(i == 0)
    def _():
        zeros = jnp.zeros((BLOCK, KV_W), BF16)
        kp_s[0:BLOCK, :] = zeros
        vp_s[0:BLOCK, :] = zeros
        kp_s[BLOCK + DEC_SEQ:, :] = zeros
        vp_s[BLOCK + DEC_SEQ:, :] = zeros
        kp_s[BLOCK:BLOCK + DEC_SEQ, :] = _rope(k_ref[...], cos_all_ref[...], sin_all_ref[...]).astype(BF16)
        vp_s[BLOCK:BLOCK + DEC_SEQ, :] = v_ref[...].astype(BF16)

    start = pl.multiple_of(i * BLOCK, BLOCK)
    kw = kp_s[pl.ds(start, 3 * BLOCK), :]
    vw = vp_s[pl.ds(start, 3 * BLOCK), :]
    ck = ck_ref[...].astype(BF16)
    cv = cv_ref[...].astype(BF16)
    qi = lax.broadcasted_iota(jnp.int32, (BLOCK, 3 * BLOCK), 0)
    ki = lax.broadcasted_iota(jnp.int32, (BLOCK, 3 * BLOCK), 1)
    rel = ki - BLOCK - qi
    kpos = ki + (start - BLOCK)
    ok = (jnp.abs(rel) <= WINDOW) & (kpos >= 0) & (kpos < DEC_SEQ)
    cq = cq_ref[...]
    sq = sq_ref[...]
    for pair in range(N_HEADS // 2):
        q2 = _rope(q_ref[:, pair * 128:(pair + 1) * 128], cq, sq).astype(BF16)
        for sub in range(2):
            hd = pair * 2 + sub
            kv = hd // GQA_G
            sk = sink_ref[hd]
            qh = q2[:, sub * HEAD_DIM:(sub + 1) * HEAD_DIM]
            sw = _dot_nt(qh, kw[:, kv * HEAD_DIM:(kv + 1) * HEAD_DIM]) * SCALE
            sw = jnp.where(ok, sw, NEG)
            sc = _dot_nt(qh, ck[:, kv * HEAD_DIM:(kv + 1) * HEAD_DIM]) * SCALE
            m = jnp.maximum(jnp.maximum(jnp.max(sw, axis=-1, keepdims=True),
                                        jnp.max(sc, axis=-1, keepdims=True)), sk)
            ew = jnp.exp(sw - m)
            ec = jnp.exp(sc - m)
            den = (jnp.sum(ew, axis=-1, keepdims=True) + jnp.sum(ec, axis=-1, keepdims=True)
                   + jnp.exp(sk - m))
            o = (_dot(ew.astype(BF16), vw[:, kv * HEAD_DIM:(kv + 1) * HEAD_DIM])
                 + _dot(ec.astype(BF16), cv[:, kv * HEAD_DIM:(kv + 1) * HEAD_DIM]))
            o_ref[:, hd * HEAD_DIM:(hd + 1) * HEAD_DIM] = o / den


def _attn_lat(p, sink, cache_k, cache_v):
    cos, sin = _rope_tables()
    nqb = DEC_SEQ // BLOCK
    q0 = N_CTX // BLOCK
    b0 = N_CTX // DEC_SEQ
    kcol = ATT_W // KV_W
    return pl.pallas_call(
        _attn_lat_kernel,
        grid=(DEC_BATCH, nqb),
        in_specs=[
            pl.BlockSpec(memory_space=pltpu.SMEM),
            pl.BlockSpec((BLOCK, ATT_W), lambda b, i: (q0 + b * nqb + i, 0)),
            pl.BlockSpec((DEC_SEQ, KV_W), lambda b, i: (b0 + b, kcol)),
            pl.BlockSpec((DEC_SEQ, KV_W), lambda b, i: (b0 + b, kcol + 1)),
            pl.BlockSpec((None, PAST_LEN, KV_W), lambda b, i: (b, 0, 0)),
            pl.BlockSpec((None, PAST_LEN, KV_W), lambda b, i: (b, 0, 0)),
            pl.BlockSpec((BLOCK, 128), lambda b, i: (i, 0)),
            pl.BlockSpec((BLOCK, 128), lambda b, i: (i, 0)),
            pl.BlockSpec((DEC_SEQ, 128), lambda b, i: (0, 0)),
            pl.BlockSpec((DEC_SEQ, 128), lambda b, i: (0, 0)),
        ],
        out_specs=pl.BlockSpec((BLOCK, ATT_W), lambda b, i: (b * nqb + i, 0)),
        out_shape=jax.ShapeDtypeStruct((N_LAT, ATT_W), F32),
        scratch_shapes=[pltpu.VMEM((DEC_SEQ + 2 * BLOCK, KV_W), BF16),
                        pltpu.VMEM((DEC_SEQ + 2 * BLOCK, KV_W), BF16)],
        compiler_params=_params("arbitrary", "arbitrary"),
        name="attn_lat",
    )(sink, p, p, p, cache_k, cache_v, cos, sin, cos, sin)


def _shift_rows(x, row, d):
    if d == 0:
        return x
    n = x.shape[0]
    y = pltpu.roll(x, d % n, 0)
    return jnp.where((row >= d) & (row < n + d), y, 0.0)


def _sigmoid(x):
    return 0.5 * jnp.tanh(0.5 * x) + 0.5


def _gelu_tanh(x):
    return x * (0.5 * (1.0 + jnp.tanh(math.sqrt(2.0 / math.pi) * (x + 0.044715 * (x * x * x)))))


def _lru_kernel(xr_ref, g_ref, h0_ref, cw_ref, cb_ref, wg_ref, bg_ref, lam_ref, o_ref, st_ref,
                af_s, uf_s, ab_s, ub_s, hf_s, hb_s, *, L, G):
    sp = jnp.maximum(-lam_ref[...], 0.0) + jnp.log1p(jnp.exp(-jnp.abs(lam_ref[...])))
    row = lax.broadcasted_iota(jnp.int32, (L, LCB), 0)
    left = LRU_CONV // 2

    def prep(b, carry):
        r0 = pl.multiple_of(b * L, L)
        x = xr_ref[pl.ds(r0, L), :]
        xc = cb_ref[...]
        for k in range(LRU_CONV):
            xc = xc + _shift_rows(x, row, left - k) * cw_ref[k:k + 1, :]
        gates = _dot(xc.astype(BF16), wg_ref[...]) + bg_ref[...]
        for d, (a_s, u_s) in enumerate(((af_s, uf_s), (ab_s, ub_s))):
            r = _sigmoid(gates[:, (2 * d) * LCB:(2 * d + 1) * LCB])
            gi = _sigmoid(gates[:, (2 * d + 1) * LCB:(2 * d + 2) * LCB])
            log_a = (-LRU_C * r) * sp[d:d + 1, :]
            a = jnp.exp(log_a)
            a_s[pl.ds(r0, L), :] = a
            u_s[pl.ds(r0, L), :] = jnp.sqrt(jnp.tanh(-log_a) * (1.0 + a * a)) * (gi * xc)
        return carry

    lax.fori_loop(0, G, prep, 0)

    def step(t, carry):
        hf, hb = carry
        tb = L - 1 - t
        hf = af_s[pl.ds(t, G, stride=L), :] * hf + uf_s[pl.ds(t, G, stride=L), :]
        hf_s[pl.ds(t, G, stride=L), :] = hf
        hb = ab_s[pl.ds(tb, G, stride=L), :] * hb + ub_s[pl.ds(tb, G, stride=L), :]
        hb_s[pl.ds(tb, G, stride=L), :] = hb
        return hf, hb

    hf, hb = lax.fori_loop(0, L, step, (h0_ref[0], h0_ref[1]), unroll=8)
    st_ref[0] = hf
    st_ref[1] = hb

    def fin(b, carry):
        r0 = pl.multiple_of(b * L, L)
        o_ref[pl.ds(r0, L), :] = (hf_s[pl.ds(r0, L), :] + hb_s[pl.ds(r0, L), :]) * _gelu_tanh(g_ref[pl.ds(r0, L), :])
        return carry

    lax.fori_loop(0, G, fin, 0)


def _lru(p, h0, conv_w, conv_b, wg, bg, lam, *, row0, L, B, G):
    nc = LRU_W // LCB
    xcol = (ATT_W + 2 * KV_W) // LCB
    gcol = xcol + nc
    rb0 = row0 // (G * L)
    kern = functools.partial(_lru_kernel, L=L, G=G)
    return pl.pallas_call(
        kern,
        grid=(B // G, nc),
        in_specs=[
            pl.BlockSpec((G * L, LCB), lambda i, c: (rb0 + i, xcol + c)),
            pl.BlockSpec((G * L, LCB), lambda i, c: (rb0 + i, gcol + c)),
            pl.BlockSpec((2, G, LCB), lambda i, c: (0, i, c)),
            pl.BlockSpec((LRU_CONV, LCB), lambda i, c: (0, c)),
            pl.BlockSpec((1, LCB), lambda i, c: (0, c)),
            pl.BlockSpec((None, LCB, 4 * LCB), lambda i, c: (c, 0, 0)),
            pl.BlockSpec((None, 1, 4 * LCB), lambda i, c: (c, 0, 0)),
            pl.BlockSpec((2, LCB), lambda i, c: (0, c)),
        ],
        out_specs=[
            pl.BlockSpec((G * L, LCB), lambda i, c: (i, c)),
            pl.BlockSpec((2, G, LCB), lambda i, c: (0, i, c)),
        ],
        out_shape=[jax.ShapeDtypeStruct((B * L, LRU_W), F32),
                   jax.ShapeDtypeStruct((2, B, LRU_W), F32)],
        scratch_shapes=[pltpu.VMEM((G * L, LCB), F32) for _ in range(6)],
        compiler_params=_params("arbitrary", "arbitrary"),
        name="lru",
    )(p, p, h0, conv_w, conv_b.reshape(1, LRU_W), wg, bg, lam)


def _lru_gate_weights(w_r, b_r, w_i, b_i):
    nc = LRU_W // LCB
    per = LCB // LRU_BD
    eye = jnp.eye(per, dtype=F32)

    def dense(w):
        w = w.reshape(nc, per, LRU_BD, LRU_BD)
        d = jnp.einsum('chij,hg->chigj', w, eye)
        return d.reshape(nc, LCB, LCB)

    cols = [dense(w_r[0]), dense(w_i[0]), dense(w_r[1]), dense(w_i[1])]
    wg = jnp.concatenate(cols, axis=-1).astype(BF16)
    bias = [b.reshape(nc, 1, LCB) for b in (b_r[0], b_i[0], b_r[1], b_i[1])]
    return wg, jnp.concatenate(bias, axis=-1)


def _fnet_kernel(f_ref, seq_ref, grp_ref, o_ref, cs_s, *, L, G):
    grp = grp_ref[...]
    for b in range(G):
        for gi in range(FNET_GROUPS):
            f = f_ref[b * L:(b + 1) * L, gi * FNET_GD:(gi + 1) * FNET_GD].astype(BF16)
            cs = _dot(f, grp).astype(BF16)
            col = b * FNET_W + gi * FNET_GD
            cs_s[0:L, col:col + FNET_GD] = cs[:, :FNET_GD]
            cs_s[L:2 * L, col:col + FNET_GD] = cs[:, FNET_GD:]
    y = _dot(seq_ref[...], cs_s[...])
    for b in range(G):
        o_ref[b * L:(b + 1) * L, :] = y[:, b * FNET_W:(b + 1) * FNET_W]


def _fnet(p, *, row0, L, B, G):
    seq, grp = _fnet_tables(L)
    rb0 = row0 // (G * L)
    return pl.pallas_call(
        functools.partial(_fnet_kernel, L=L, G=G),
        grid=(B // G,),
        in_specs=[
            pl.BlockSpec((G * L, FNET_W), lambda b: (rb0 + b, 0)),
            _resident((L, 2 * L), lambda b: (0, 0)),
            _resident((FNET_GD, 2 * FNET_GD), lambda b: (0, 0)),
        ],
        out_specs=pl.BlockSpec((G * L, FNET_W), lambda b: (b, 0)),
        out_shape=jax.ShapeDtypeStruct((B * L, FNET_W), F32),
        scratch_shapes=[pltpu.VMEM((2 * L, G * FNET_W), BF16)],
        compiler_params=_params("arbitrary"),
        name="fnet",
    )(p, _table_bf16(seq), _table_bf16(grp))


def _alternating_sign(row):
    return (1 - 2 * (row & 1)).astype(F32)


def _filter_kernel(zh_ref, zl_ref, w1_ref, b1_ref, w2_ref, b2_ref, w3f_ref, w3b_ref, fr_ref, ldf_ref, ldb_ref,
                   fc_ref, fs_ref, fcb_ref, fsb_ref, k1_ref, k2_ref, kn_ref, *, L):
    hid = jnp.sin(fr_ref[0:1, :] * (_dot3((zh_ref[...], zl_ref[...]), _split(w1_ref[...])) + b1_ref[...]))
    hid = jnp.sin(fr_ref[1:2, :] * (_dot3(_split(hid), _split(w2_ref[...])) + b2_ref[...]))
    hid = _split(hid)
    tn = lax.broadcasted_iota(jnp.int32, (L, CB), 0).astype(F32) * (1.0 / L)
    ff = _dot3(hid, _split(w3f_ref[...])) * jnp.exp(-tn * jnp.exp(ldf_ref[...]))
    fb = _dot3(hid, _split(w3b_ref[...])) * jnp.exp(-tn * jnp.exp(ldb_ref[...]))
    ss = jnp.sum(ff * ff, axis=0, keepdims=True) + jnp.sum(fb * fb, axis=0, keepdims=True)
    scale = lax.rsqrt(ss + EPS)
    row = lax.broadcasted_iota(jnp.int32, (L, CB), 0)
    sgn = _alternating_sign(row)
    nyq = jnp.sum(sgn * ff, axis=0, keepdims=True) - jnp.sum(sgn * fb, axis=0, keepdims=True)
    ff = ff.astype(BF16)
    fb = fb.astype(BF16)
    kr = _dot(fc_ref[...], ff) + _dot(fcb_ref[...], fb)
    ki = _dot(fs_ref[...], ff) + _dot(fsb_ref[...], fb)
    w = jnp.where(row == 0, 0.5 / L, 1.0 / L) * scale
    k1_ref[0] = kr * w
    k2_ref[0] = ki * w
    kn_ref[0] = nyq * (scale * (0.5 / L))


def _hyena_filters(L, w1, b1, w2, b2, w3, freq, log_decay):
    zh, zl = _hyena_embedding(L)
    tabs = [_table_bf16(t) for t in _rdft_tables(L)]
    nc = HY_W // CB
    w1p = jnp.zeros((EMB_PAD, HY_HID), F32).at[:HY_EMB].set(w1)
    bcol = HY_ORDER * nc
    tab_spec = _resident((L, L), lambda n, c: (0, 0))
    small = lambda shape: _resident(shape, lambda n, c: (0, 0))
    return pl.pallas_call(
        functools.partial(_filter_kernel, L=L),
        grid=(HY_ORDER, nc),
        in_specs=[
            small((L, EMB_PAD)), small((L, EMB_PAD)),
            small((EMB_PAD, HY_HID)), small((1, HY_HID)),
            small((HY_HID, HY_HID)), small((1, HY_HID)),
            pl.BlockSpec((HY_HID, CB), lambda n, c: (0, n * nc + c)),
            pl.BlockSpec((HY_HID, CB), lambda n, c: (0, bcol + n * nc + c)),
            small((2, HY_HID)),
            pl.BlockSpec((1, CB), lambda n, c: (0, n * nc + c)),
            pl.BlockSpec((1, CB), lambda n, c: (0, bcol + n * nc + c)),
        ] + [tab_spec] * 4,
        out_specs=[
            pl.BlockSpec((1, L, CB), lambda n, c: (n, 0, c)),
            pl.BlockSpec((1, L, CB), lambda n, c: (n, 0, c)),
            pl.BlockSpec((1, 1, CB), lambda n, c: (n, 0, c)),
        ],
        out_shape=[jax.ShapeDtypeStruct((HY_ORDER, L, HY_W), F32),
                   jax.ShapeDtypeStruct((HY_ORDER, L, HY_W), F32),
                   jax.ShapeDtypeStruct((HY_ORDER, 1, HY_W), F32)],
        compiler_params=_params("arbitrary", "arbitrary"),
        name="hyena_filter",
    )(zh, zl, w1p, b1.reshape(1, HY_HID), w2, b2.reshape(1, HY_HID), w3, w3, freq,
      log_decay.reshape(1, HY_FILT), log_decay.reshape(1, HY_FILT), *tabs)


def _hyena_kernel(v_ref, x1_ref, x2_ref, cwv_ref, cw1_ref, cw2_ref, cbv_ref, cb1_ref, cb2_ref, bias_ref,
                  k1_ref, k2_ref, kn_ref, fc_ref, fs_ref, o_ref, *, L, G):
    row = lax.broadcasted_iota(jnp.int32, (L, CB), 0)
    left = HY_CONV // 2

    def conv(x_ref, cw_ref, cb_ref):
        outs = []
        for g in range(G):
            x = x_ref[g * L:(g + 1) * L, :]
            y = cb_ref[...]
            for k in range(HY_CONV):
                y = y + _shift_rows(x, row, left - k) * cw_ref[k:k + 1, :]
            outs.append(y)
        return outs[0] if G == 1 else jnp.concatenate(outs, axis=1)

    def per_seq(x):
        return x if G == 1 else jnp.concatenate([x] * G, axis=1)

    fc = fc_ref[...]
    fs = fs_ref[...]
    sgn = per_seq(_alternating_sign(row))
    z = conv(v_ref, cwv_ref, cbv_ref)
    gates = (conv(x1_ref, cw1_ref, cb1_ref), conv(x2_ref, cw2_ref, cb2_ref))
    for n in range(HY_ORDER):
        zb = z.astype(BF16)
        a = _dot(fc, zb)
        b = _dot(fs, zb)
        nyq = jnp.sum(sgn * z, axis=0, keepdims=True)
        k1 = per_seq(k1_ref[n])
        k2 = per_seq(k2_ref[n])
        p = a * k1 - b * k2
        q = a * k2 + b * k1
        y = _dot(fc, p.astype(BF16)) + _dot(fs, q.astype(BF16)) + sgn * (nyq * per_seq(kn_ref[n]))
        z = gates[n] * (y + per_seq(bias_ref[n:n + 1, :]) * z)
    for g in range(G):
        o_ref[g * L:(g + 1) * L, :] = z[:, g * CB:(g + 1) * CB]


def _hyena(p, conv_w, conv_b, hy_bias, k1, k2, kn, *, row0, L, B, G):
    fc, fs = (_table_bf16(t) for t in _rdft_tables(L)[:2])
    nc = HY_W // CB
    rb0 = row0 // (G * L)
    c0 = FNET_W // CB
    conv_b = conv_b.reshape(1, (HY_ORDER + 1) * HY_W)
    x_spec = lambda j: pl.BlockSpec((G * L, CB), lambda c, b: (rb0 + b, c0 + j * nc + c))
    cw_spec = lambda j: pl.BlockSpec((HY_CONV, CB), lambda c, b: (0, j * nc + c))
    cb_spec = lambda j: pl.BlockSpec((1, CB), lambda c, b: (0, j * nc + c))
    k_spec = pl.BlockSpec((HY_ORDER, L, CB), lambda c, b: (0, 0, c))
    return pl.pallas_call(
        functools.partial(_hyena_kernel, L=L, G=G),
        grid=(nc, B // G),
        in_specs=[x_spec(0), x_spec(1), x_spec(2), cw_spec(0), cw_spec(1), cw_spec(2),
                  cb_spec(0), cb_spec(1), cb_spec(2),
                  pl.BlockSpec((HY_ORDER, CB), lambda c, b: (0, c)),
                  k_spec, k_spec, pl.BlockSpec((HY_ORDER, 1, CB), lambda c, b: (0, 0, c))]
                 + [_resident((L, L), lambda c, b: (0, 0))] * 2,
        out_specs=pl.BlockSpec((G * L, CB), lambda c, b: (b, c)),
        out_shape=jax.ShapeDtypeStruct((B * L, HY_W), F32),
        compiler_params=_params("arbitrary", "arbitrary"),
        name="hyena",
    )(p, p, p, conv_w, conv_w, conv_w, conv_b, conv_b, conv_b, hy_bias, k1, k2, kn, fc, fs)


def kernel(x_prompt, x_sample, c, cache_k, cache_v, state_lru, c_ctx, mod_w, mod_b, norm_mix, norm_mlp, norm_final, mlp_w1, mlp_w2, ev_w_in, ev_w_out, attn_sink, lru_conv_w, lru_conv_b, lru_w_r, lru_b_r, lru_w_i, lru_b_i, lru_lambda, od_w_in, od_w_out, hy_conv_w, hy_conv_b, hy_w1, hy_b1, hy_w2, hy_b2, hy_w3, hy_freq, hy_log_decay, hy_bias):
    x = (x_prompt.reshape(N_CTX, D_MODEL), x_sample.reshape(N_LAT, D_MODEL))
    cvec = jnp.concatenate([c_ctx[None, :], c, jnp.zeros((8 - N_GROUPS, D_MODEL), F32)], axis=0)
    mods = _modulation(cvec, mod_w, mod_b)
    zero_state = jnp.zeros((2, BATCH, LRU_W), F32)
    ev_w_in_b, ev_w_out_b = ev_w_in.astype(BF16), ev_w_out.astype(BF16)
    od_w_in_b, od_w_out_b = od_w_in.astype(BF16), od_w_out.astype(BF16)
    mlp_w1_b, mlp_w2_b = mlp_w1.astype(BF16), mlp_w2.astype(BF16)
    k_list, v_list, s_list = [], [], []
    for l in range(DEPTH):
        j = l // 2
        if l % 2 == 0:
            p = _pre(x, norm_mix[l], mods[l], ev_w_in_b, j)
            att_ctx, k_ctx, v_ctx = _attn_ctx(p, attn_sink[j])
            k_list.append(k_ctx.reshape(BATCH, SEQ, N_KV, HEAD_DIM))
            v_list.append(v_ctx.reshape(BATCH, SEQ, N_KV, HEAD_DIM))
            att_lat = _attn_lat(p, attn_sink[j], cache_k[:, j].reshape(DEC_BATCH, PAST_LEN, KV_W),
                                cache_v[:, j].reshape(DEC_BATCH, PAST_LEN, KV_W))
            wg, bg = _lru_gate_weights(lru_w_r[j], lru_b_r[j], lru_w_i[j], lru_b_i[j])
            lru_args = (lru_conv_w[j], lru_conv_b[j], wg, bg, lru_lambda[j])
            y_ctx, st = _lru(p, zero_state, *lru_args, row0=0, L=SEQ, B=BATCH, G=8)
            h0_lat = jnp.swapaxes(state_lru[:, j], 0, 1)
            y_lat, _ = _lru(p, h0_lat, *lru_args, row0=N_CTX, L=DEC_SEQ, B=DEC_BATCH, G=DEC_BATCH)
            s_list.append(jnp.swapaxes(st, 0, 1))
            m1, m2 = (att_ctx, att_lat), (y_ctx, y_lat)
            w_out = ev_w_out_b
        else:
            p = _pre(x, norm_mix[l], mods[l], od_w_in_b, j)
            filt = (hy_w1[j], hy_b1[j], hy_w2[j], hy_b2[j], hy_w3[j], hy_freq[j], hy_log_decay[j])
            m1 = (_fnet(p, row0=0, L=SEQ, B=BATCH, G=4), _fnet(p, row0=N_CTX, L=DEC_SEQ, B=DEC_BATCH, G=1))
            hy = []
            for row0, L, B, G in ((0, SEQ, BATCH, 4), (N_CTX, DEC_SEQ, DEC_BATCH, 1)):
                k1, k2, kn = _hyena_filters(L, *filt)
                hy.append(_hyena(p, hy_conv_w[j], hy_conv_b[j], hy_bias[j], k1, k2, kn, row0=row0, L=L, B=B, G=G))
            m2 = tuple(hy)
            w_out = od_w_out_b
        post = functools.partial(_post, x, m1, m2, mods[l], norm_mlp[l], norm_final, w_out, mlp_w1_b, mlp_w2_b, j, l)
        if l < DEPTH - 1:
            x = post(final=False)
        else:
            y_prompt = post(final=True, tile0=0, n_tiles=N_CTX_TILES).reshape(BATCH, SEQ, D_MODEL)
            y_sample = post(final=True, tile0=N_CTX_TILES, n_tiles=N_LAT // TM).reshape(DEC_BATCH, DEC_SEQ, D_MODEL)
    k_state = jnp.stack(k_list, axis=1)
    v_state = jnp.stack(v_list, axis=1)
    lru_state = jnp.stack(s_list, axis=1).astype(x_prompt.dtype)
    return (y_prompt, y_sample, k_state, v_state, lru_state)
```

```python
import functools
import math

import numpy as np
import jax
import jax.numpy as jnp
from jax import lax
from jax.experimental import pallas as pl
from jax.experimental.pallas import tpu as pltpu

F32 = jnp.float32
BF16 = jnp.bfloat16

D_MODEL = 1024
BATCH = 32
SEQ = 256
DEPTH = 4
DEC_BATCH = 2
DEC_SEQ = 1024
PAST_LEN = 256
GRID_W = 64
N_HEADS = 8
N_KV = 2
HEAD_DIM = 64
GQA_G = N_HEADS // N_KV
ATT_W = N_HEADS * HEAD_DIM
KV_W = N_KV * HEAD_DIM
WINDOW = 128
BLOCK = 128
ROPE_BASE = 10000.0
LRU_W = 512
LRU_BLOCKS = 8
LRU_BD = LRU_W // LRU_BLOCKS
LRU_CONV = 4
LRU_C = 8.0
FNET_GROUPS = 4
FNET_W = 512
FNET_GD = FNET_W // FNET_GROUPS
HY_W = 512
HY_ORDER = 2
HY_CONV = 3
HY_BANDS = 16
HY_EMB = 1 + 2 * HY_BANDS
HY_HID = 64
HY_FILT = 2 * HY_ORDER * HY_W
EVEN_IN = ATT_W + 2 * KV_W + 2 * LRU_W
ODD_IN = FNET_W + (HY_ORDER + 1) * HY_W
D_FF = 4 * D_MODEL
N_MOD = 6
EPS = 1e-6
NEG = -1e30

N_CTX = BATCH * SEQ
N_LAT = DEC_BATCH * DEC_SEQ
N_TOK = N_CTX + N_LAT
N_GROUPS = 1 + DEC_BATCH

VMEM_LIMIT = 56 * 1024 * 1024
TM = 512
CB = 256
LCB = 128
FF_CHUNK = 1024
EMB_PAD = 128


def _params(*sem):
    return pltpu.CompilerParams(dimension_semantics=sem, vmem_limit_bytes=VMEM_LIMIT)


def _resident(shape, index_map):
    return pl.BlockSpec(shape, index_map, pipeline_mode=pl.Buffered(1))


def _dot(a, b):
    return jnp.dot(a, b, preferred_element_type=F32)


def _dot_nt(a, b):
    return lax.dot_general(a, b, (((1,), (1,)), ((), ())), preferred_element_type=F32)


def _split(x):
    hi = x.astype(BF16)
    lo = (x - hi.astype(F32)).astype(BF16)
    return hi, lo


def _dot3(a, b):
    return _dot(a[0], b[0]) + (_dot(a[1], b[0]) + _dot(a[0], b[1]))


def _np_f32(x):
    return np.asarray(x, dtype=np.float32)


def _table_bf16(t):
    return jnp.asarray(t).astype(BF16)


def _np_split(x):
    hi = np.asarray(x, dtype=np.float32).astype(BF16)
    lo = (np.asarray(x, dtype=np.float64) - hi.astype(np.float64)).astype(np.float32).astype(BF16)
    return hi, lo


def _group_of_tile(i):
    return jnp.maximum(i * TM - (N_CTX - DEC_SEQ), 0) // DEC_SEQ


@functools.lru_cache(maxsize=None)
def _rdft_tables(L):
    k = np.arange(L, dtype=np.int64)[:, None]
    t = np.arange(L, dtype=np.int64)[None, :]
    ang = (k * t % (2 * L)).astype(np.float64) * (math.pi / L)
    tb = 2 * L - 1 - t
    angb = (k * tb % (2 * L)).astype(np.float64) * (math.pi / L)
    return tuple(_np_f32(a) for a in (np.cos(ang), -np.sin(ang), np.cos(angb), -np.sin(angb)))


@functools.lru_cache(maxsize=None)
def _fnet_tables(L):
    k = np.arange(L, dtype=np.int64)[:, None]
    t = np.arange(L, dtype=np.int64)[None, :]
    ang = (k * t % L).astype(np.float64) * (2.0 * math.pi / L)
    scale = 1.0 / math.sqrt(L * FNET_GD)
    seq = np.concatenate([np.cos(ang), -np.sin(ang)], axis=1) * scale
    m = np.arange(FNET_GD, dtype=np.int64)[:, None]
    d = np.arange(FNET_GD, dtype=np.int64)[None, :]
    angg = (m * d % FNET_GD).astype(np.float64) * (2.0 * math.pi / FNET_GD)
    grp = np.concatenate([np.cos(angg), np.sin(angg)], axis=1)
    return _np_f32(seq), _np_f32(grp)


@functools.lru_cache(maxsize=None)
def _rope_tables():
    L = DEC_SEQ
    n = HEAD_DIM // 4
    row = np.repeat(np.arange(L // GRID_W), GRID_W).astype(np.float32)
    col = np.tile(np.arange(GRID_W), L // GRID_W).astype(np.float32)
    inv = (np.float32(ROPE_BASE) ** (-np.arange(n, dtype=np.float32) / np.float32(n))).astype(np.float32)
    ar = (row[:, None] * inv[None, :]).astype(np.float64)
    ac = (col[:, None] * inv[None, :]).astype(np.float64)
    cos = np.concatenate([np.cos(ar), np.cos(ar), np.cos(ac), np.cos(ac)], axis=1)
    sin = np.concatenate([-np.sin(ar), np.sin(ar), -np.sin(ac), np.sin(ac)], axis=1)
    cos = np.tile(cos, (1, 2)).astype(np.float32)
    sin = np.tile(sin, (1, 2)).astype(np.float32)
    return cos, sin


@functools.lru_cache(maxsize=None)
def _hyena_embedding(L):
    t = np.arange(L, dtype=np.float32)
    tn = (t / np.float32(L)).astype(np.float64)
    bands = np.linspace(1e-4, HY_BANDS - 1, HY_BANDS, dtype=np.float32).astype(np.float64)
    w = (np.float32(2.0 * math.pi / L) * t).astype(np.float64)
    z = np.concatenate([tn[:, None], np.cos(w[:, None] * bands), -np.sin(w[:, None] * bands)], axis=-1)
    zp = np.zeros((L, EMB_PAD), np.float64)
    zp[:, :HY_EMB] = z
    return _np_split(zp)


MOD_TN = 1536


def _mod_kernel(c_ref, w_ref, b_ref, o_ref):
    c = c_ref[...]
    s = c * jax.nn.sigmoid(c)
    o_ref[0] = _dot3(_split(s), _split(w_ref[0])) + b_ref[0]


def _modulation(cvec, mod_w, mod_b):
    out = pl.pallas_call(
        _mod_kernel,
        grid=(DEPTH, N_MOD * D_MODEL // MOD_TN),
        in_specs=[
            pl.BlockSpec((8, D_MODEL), lambda l, n: (0, 0)),
            pl.BlockSpec((1, D_MODEL, MOD_TN), lambda l, n: (l, 0, n)),
            pl.BlockSpec((1, 1, MOD_TN), lambda l, n: (l, 0, n)),
        ],
        out_specs=pl.BlockSpec((1, 8, MOD_TN), lambda l, n: (l, 0, n)),
        out_shape=jax.ShapeDtypeStruct((DEPTH, 8, N_MOD * D_MODEL), F32),
        compiler_params=_params("arbitrary", "arbitrary"),
        name="modulation",
    )(cvec, mod_w, mod_b.reshape(DEPTH, 1, N_MOD * D_MODEL))
    return out.reshape(DEPTH, 8, N_MOD, D_MODEL)[:, :N_GROUPS]


def _rms(x, g):
    return x * lax.rsqrt(jnp.mean(x * x, axis=-1, keepdims=True) + EPS) * g


N_CTX_TILES = N_CTX // TM


def _row_stream(x, tile0):
    if not isinstance(x, tuple):
        return [x], [pl.BlockSpec((TM, x.shape[1]), lambda i: (i + tile0, 0))]
    ctx, lat = x
    return [ctx, lat], [
        pl.BlockSpec((TM, ctx.shape[1]), lambda i: (jnp.minimum(i + tile0, N_CTX_TILES - 1), 0)),
        pl.BlockSpec((TM, lat.shape[1]), lambda i: (jnp.maximum(i + tile0 - N_CTX_TILES, 0), 0)),
    ]


def _read_rows(refs, paired, tile):
    vals = []
    k = 0
    for p in paired:
        if p:
            vals.append(jnp.where(tile < N_CTX_TILES, refs[k][...], refs[k + 1][...]))
            k += 2
        else:
            vals.append(refs[k][...])
            k += 1
    return vals, refs[k:]


def _pre_kernel(*refs, paired, tile0):
    (x,), (g_ref, mod_ref, w_ref, o_ref) = _read_rows(refs, paired, pl.program_id(0) + tile0)
    h = _rms(x, g_ref[...]) * (1.0 + mod_ref[1:2, :]) + mod_ref[0:1, :]
    o_ref[...] = _dot(h.astype(BF16), w_ref[...])


def _pre(x, g, mod, w_in, j):
    n_in = w_in.shape[2]
    ops, specs = _row_stream(x, 0)
    return pl.pallas_call(
        functools.partial(_pre_kernel, paired=(isinstance(x, tuple),), tile0=0),
        grid=(N_TOK // TM,),
        in_specs=specs + [
            _resident((1, D_MODEL), lambda i: (0, 0)),
            pl.BlockSpec((None, N_MOD, D_MODEL), lambda i: (_group_of_tile(i), 0, 0)),
            _resident((None, D_MODEL, n_in), lambda i: (j, 0, 0)),
        ],
        out_specs=pl.BlockSpec((TM, n_in), lambda i: (i, 0)),
        out_shape=jax.ShapeDtypeStruct((N_TOK, n_in), F32),
        compiler_params=_params("arbitrary"),
        name="pre",
    )(*ops, g.reshape(1, D_MODEL), mod, w_in)


def _post_kernel(*refs, paired, tile0, final):
    (x, m1, m2), rest = _read_rows(refs, paired, pl.program_id(0) + tile0)
    mod_ref, modn_ref, g_ref, gf_ref, wo_ref, w1_ref, w2_ref = rest[:7]
    if final:
        (o_ref,) = rest[7:]
    else:
        w1f_ref, w2f_ref, o_ref, hn_ref, w1n_ref, w2n_ref = rest[7:]
    half = wo_ref.shape[0] // 2
    o = _dot(m1.astype(BF16), wo_ref[:half, :]) + _dot(m2.astype(BF16), wo_ref[half:, :])
    x1 = x + mod_ref[2:3, :] * o
    h = (_rms(x1, g_ref[...]) * (1.0 + mod_ref[4:5, :]) + mod_ref[3:4, :]).astype(BF16)
    acc = jnp.zeros((TM, D_MODEL), F32)
    for c in range(D_FF // FF_CHUNK):
        t = jnp.maximum(_dot(h, w1_ref[:, c * FF_CHUNK:(c + 1) * FF_CHUNK]), 0.0)
        acc = acc + _dot((t * t).astype(BF16), w2_ref[c * FF_CHUNK:(c + 1) * FF_CHUNK, :])
    x2 = x1 + mod_ref[5:6, :] * acc
    if final:
        o_ref[...] = _rms(x2, gf_ref[...])
    else:
        o_ref[...] = x2
        hn_ref[...] = (_rms(x2, gf_ref[...]) * (1.0 + modn_ref[1:2, :]) + modn_ref[0:1, :]).astype(BF16)
        w1n_ref[...] = w1f_ref[...].astype(BF16)
        w2n_ref[...] = w2f_ref[...].astype(BF16)


W_CHUNKS = 16


def _post(x, m1, m2, mod, mod_next, g, g_tail, w_out, w1, w2, w1_f32, w2_f32, j, l, *, final, tile0=0,
          n_tiles=N_TOK // TM):
    mod_spec = pl.BlockSpec((None, N_MOD, D_MODEL), lambda i: (_group_of_tile(i + tile0), 0, 0))
    row_spec = pl.BlockSpec((TM, D_MODEL), lambda i: (i, 0))
    x_shape = jax.ShapeDtypeStruct((n_tiles * TM, D_MODEL), F32)
    extra_in, extra_ops, extra_out, extra_shapes = [], [], [], []
    if not final:
        assert n_tiles >= W_CHUNKS
        cw = D_FF // W_CHUNKS
        chunk = lambda i: jnp.minimum(i, W_CHUNKS - 1)
        extra_in = [pl.BlockSpec((None, D_MODEL, cw), lambda i: (l + 1, 0, chunk(i))),
                    pl.BlockSpec((None, cw, D_MODEL), lambda i: (l + 1, chunk(i), 0))]
        extra_ops = [w1_f32, w2_f32]
        extra_out = [row_spec, pl.BlockSpec((D_MODEL, cw), lambda i: (0, chunk(i))),
                     pl.BlockSpec((cw, D_MODEL), lambda i: (chunk(i), 0))]
        extra_shapes = [jax.ShapeDtypeStruct((n_tiles * TM, D_MODEL), BF16),
                        jax.ShapeDtypeStruct((D_MODEL, D_FF), BF16), jax.ShapeDtypeStruct((D_FF, D_MODEL), BF16)]
    ops, specs, paired = [], [], []
    for s in (x, m1, m2):
        o, sp = _row_stream(s, tile0)
        ops += o
        specs += sp
        paired.append(isinstance(s, tuple))
    return pl.pallas_call(
        functools.partial(_post_kernel, paired=tuple(paired), tile0=tile0, final=final),
        grid=(n_tiles,),
        in_specs=specs + [
            mod_spec,
            mod_spec,
            _resident((1, D_MODEL), lambda i: (0, 0)),
            _resident((1, D_MODEL), lambda i: (0, 0)),
            _resident((None,) + w_out.shape[1:], lambda i: (j, 0, 0)),
            _resident(w1.shape, lambda i: (0, 0)),
            _resident(w2.shape, lambda i: (0, 0)),
        ] + extra_in,
        out_specs=row_spec if final else [row_spec] + extra_out,
        out_shape=x_shape if final else [x_shape] + extra_shapes,
        compiler_params=_params("arbitrary"),
        name="post",
    )(*ops, mod, mod_next, g.reshape(1, D_MODEL), g_tail.reshape(1, D_MODEL), w_out, w1, w2, *extra_ops)


def _project_kernel(h_ref, w_ref, o_ref):
    o_ref[...] = _dot(h_ref[...], w_ref[...])


def _project(h, w_in, j):
    n_in = w_in.shape[2]
    return pl.pallas_call(
        _project_kernel,
        grid=(N_TOK // TM,),
        in_specs=[pl.BlockSpec((TM, D_MODEL), lambda i: (i, 0)),
                  _resident((None, D_MODEL, n_in), lambda i: (j, 0, 0))],
        out_specs=pl.BlockSpec((TM, n_in), lambda i: (i, 0)),
        out_shape=jax.ShapeDtypeStruct((N_TOK, n_in), F32),
        compiler_params=_params("arbitrary"),
        name="project",
    )(h, w_in)


SCALE = HEAD_DIM ** -0.5


def _attn_ctx_kernel(sink_ref, q_ref, k_ref, v_ref, o_ref, ko_ref, vo_ref):
    ko_ref[...] = k_ref[...]
    vo_ref[...] = v_ref[...]
    k = k_ref[...].astype(BF16)
    v = v_ref[...].astype(BF16)
    for hd in range(N_HEADS):
        kv = hd // GQA_G
        sk = sink_ref[hd]
        qh = q_ref[:, hd * HEAD_DIM:(hd + 1) * HEAD_DIM].astype(BF16)
        s = _dot_nt(qh, k[:, kv * HEAD_DIM:(kv + 1) * HEAD_DIM]) * SCALE
        m = jnp.maximum(jnp.max(s, axis=-1, keepdims=True), sk)
        e = jnp.exp(s - m)
        den = jnp.sum(e, axis=-1, keepdims=True) + jnp.exp(sk - m)
        o = _dot(e.astype(BF16), v[:, kv * HEAD_DIM:(kv + 1) * HEAD_DIM])
        o_ref[:, hd * HEAD_DIM:(hd + 1) * HEAD_DIM] = o / den


def _attn_ctx(p, sink):
    kcol = ATT_W // KV_W
    return pl.pallas_call(
        _attn_ctx_kernel,
        grid=(BATCH,),
        in_specs=[
            pl.BlockSpec(memory_space=pltpu.SMEM),
            pl.BlockSpec((SEQ, ATT_W), lambda b: (b, 0)),
            pl.BlockSpec((SEQ, KV_W), lambda b: (b, kcol)),
            pl.BlockSpec((SEQ, KV_W), lambda b: (b, kcol + 1)),
        ],
        out_specs=[pl.BlockSpec((SEQ, ATT_W), lambda b: (b, 0)),
                   pl.BlockSpec((SEQ, KV_W), lambda b: (b, 0)),
                   pl.BlockSpec((SEQ, KV_W), lambda b: (b, 0))],
        out_shape=[jax.ShapeDtypeStruct((N_CTX, ATT_W), F32),
                   jax.ShapeDtypeStruct((N_CTX, KV_W), F32),
                   jax.ShapeDtypeStruct((N_CTX, KV_W), F32)],
        compiler_params=_params("arbitrary"),
        name="attn_ctx",
    )(sink, p, p, p)


def _rope(x, cos, sin):
    lane = lax.broadcasted_iota(jnp.int32, x.shape, 1)
    swapped = jnp.where(lane % 32 < 16, pltpu.roll(x, 128 - 16, 1), pltpu.roll(x, 16, 1))
    return x * cos + swapped * sin


def _attn_lat_kernel(sink_ref, q_ref, k_ref, v_ref, ck_ref, cv_ref, cq_ref, sq_ref, cos_all_ref, sin_all_ref,
                     o_ref, kp_s, vp_s):
    i = pl.program_id(1)

    @pl.when(i == 0)
    def _():
        zeros = jnp.zeros((BLOCK, KV_W), BF16)
        kp_s[0:BLOCK, :] = zeros
        vp_s[0:BLOCK, :] = zeros
        kp_s[BLOCK + DEC_SEQ:, :] = zeros
        vp_s[BLOCK + DEC_SEQ:, :] = zeros
        kp_s[BLOCK:BLOCK + DEC_SEQ, :] = _rope(k_ref[...], cos_all_ref[...], sin_all_ref[...]).astype(BF16)
        vp_s[BLOCK:BLOCK + DEC_SEQ, :] = v_ref[...].astype(BF16)

    start = pl.multiple_of(i * BLOCK, BLOCK)
    kw = kp_s[pl.ds(start, 3 * BLOCK), :]
    vw = vp_s[pl.ds(start, 3 * BLOCK), :]
    ck = ck_ref[...].astype(BF16)
    cv = cv_ref[...].astype(BF16)
    qi = lax.broadcasted_iota(jnp.int32, (BLOCK, 3 * BLOCK), 0)
    ki = lax.broadcasted_iota(jnp.int32, (BLOCK, 3 * BLOCK), 1)
    rel = ki - BLOCK - qi
    kpos = ki + (start - BLOCK)
    ok = (jnp.abs(rel) <= WINDOW) & (kpos >= 0) & (kpos < DEC_SEQ)
    cq = cq_ref[...]
    sq = sq_ref[...]
    for pair in range(N_HEADS // 2):
        q2 = _rope(q_ref[:, pair * 128:(pair + 1) * 128], cq, sq).astype(BF16)
        for sub in range(2):
            hd = pair * 2 + sub
            kv = hd // GQA_G
            sk = sink_ref[hd]
            qh = q2[:, sub * HEAD_DIM:(sub + 1) * HEAD_DIM]
            sw = _dot_nt(qh, kw[:, kv * HEAD_DIM:(kv + 1) * HEAD_DIM]) * SCALE
            sw = jnp.where(ok, sw, NEG)
            sc = _dot_nt(qh, ck[:, kv * HEAD_DIM:(kv + 1) * HEAD_DIM]) * SCALE
            m = jnp.maximum(jnp.maximum(jnp.max(sw, axis=-1, keepdims=True),
                                        jnp.max(sc, axis=-1, keepdims=True)), sk)
            ew = jnp.exp(sw - m)
            ec = jnp.exp(sc - m)
            den = (jnp.sum(ew, axis=-1, keepdims=True) + jnp.sum(ec, axis=-1, keepdims=True)
                   + jnp.exp(sk - m))
            o = (_dot(ew.astype(BF16), vw[:, kv * HEAD_DIM:(kv + 1) * HEAD_DIM])
                 + _dot(ec.astype(BF16), cv[:, kv * HEAD_DIM:(kv + 1) * HEAD_DIM]))
            o_ref[:, hd * HEAD_DIM:(hd + 1) * HEAD_DIM] = o / den


def _attn_lat(p, sink, cache_k, cache_v):
    cos, sin = _rope_tables()
    nqb = DEC_SEQ // BLOCK
    q0 = N_CTX // BLOCK
    b0 = N_CTX // DEC_SEQ
    kcol = ATT_W // KV_W
    return pl.pallas_call(
        _attn_lat_kernel,
        grid=(DEC_BATCH, nqb),
        in_specs=[
            pl.BlockSpec(memory_space=pltpu.SMEM),
            pl.BlockSpec((BLOCK, ATT_W), lambda b, i: (q0 + b * nqb + i, 0)),
            pl.BlockSpec((DEC_SEQ, KV_W), lambda b, i: (b0 + b, kcol)),
            pl.BlockSpec((DEC_SEQ, KV_W), lambda b, i: (b0 + b, kcol + 1)),
            pl.BlockSpec((None, PAST_LEN, KV_W), lambda b, i: (b, 0, 0)),
            pl.BlockSpec((None, PAST_LEN, KV_W), lambda b, i: (b, 0, 0)),
            pl.BlockSpec((BLOCK, 128), lambda b, i: (i, 0)),
            pl.BlockSpec((BLOCK, 128), lambda b, i: (i, 0)),
            pl.BlockSpec((DEC_SEQ, 128), lambda b, i: (0, 0)),
            pl.BlockSpec((DEC_SEQ, 128), lambda b, i: (0, 0)),
        ],
        out_specs=pl.BlockSpec((BLOCK, ATT_W), lambda b, i: (b * nqb + i, 0)),
        out_shape=jax.ShapeDtypeStruct((N_LAT, ATT_W), F32),
        scratch_shapes=[pltpu.VMEM((DEC_SEQ + 2 * BLOCK, KV_W), BF16),
                        pltpu.VMEM((DEC_SEQ + 2 * BLOCK, KV_W), BF16)],
        compiler_params=_params("arbitrary", "arbitrary"),
        name="attn_lat",
    )(sink, p, p, p, cache_k, cache_v, cos, sin, cos, sin)


def _shift_rows(x, row, d):
    if d == 0:
        return x
    n = x.shape[0]
    y = pltpu.roll(x, d % n, 0)
    return jnp.where((row >= d) & (row < n + d), y, 0.0)


def _sigmoid(x):
    return 0.5 * jnp.tanh(0.5 * x) + 0.5


def _sqrt_nonneg(x):
    return jnp.where(x > 0.0, x * lax.rsqrt(x), 0.0)


def _gelu_tanh(x):
    return x * (0.5 * (1.0 + jnp.tanh(math.sqrt(2.0 / math.pi) * (x + 0.044715 * (x * x * x)))))


def _lru_kernel(xr_ref, g_ref, h0_ref, cw_ref, cb_ref, wg_ref, bg_ref, lam_ref, o_ref, st_ref,
                af_s, uf_s, ab_s, ub_s, hf_s, hb_s, *, L, G):
    sp = jnp.maximum(-lam_ref[...], 0.0) + jnp.log1p(jnp.exp(-jnp.abs(lam_ref[...])))
    nsp = -LRU_C * sp
    row = lax.broadcasted_iota(jnp.int32, (L, LCB), 0)
    left = LRU_CONV // 2

    def prep(b, carry):
        r0 = pl.multiple_of(b * L, L)
        x = xr_ref[pl.ds(r0, L), :]
        xc = cb_ref[...]
        for k in range(LRU_CONV):
            xc = xc + _shift_rows(x, row, left - k) * cw_ref[k:k + 1, :]
        gates = _dot(xc.astype(BF16), wg_ref[...]) + bg_ref[...]
        for d, (a_s, u_s) in enumerate(((af_s, uf_s), (ab_s, ub_s))):
            r = _sigmoid(gates[:, (2 * d) * LCB:(2 * d + 1) * LCB])
            gi = _sigmoid(gates[:, (2 * d + 1) * LCB:(2 * d + 2) * LCB])
            log_a = r * nsp[d:d + 1, :]
            a = jnp.exp(log_a)
            a_s[pl.ds(r0, L), :] = a
            u_s[pl.ds(r0, L), :] = _sqrt_nonneg(jnp.tanh(-log_a) * (1.0 + a * a)) * (gi * xc)
        return carry

    lax.fori_loop(0, G, prep, 0)

    def step(t, carry):
        hf, hb = carry
        tb = L - 1 - t
        hf = af_s[pl.ds(t, G, stride=L), :] * hf + uf_s[pl.ds(t, G, stride=L), :]
        hf_s[pl.ds(t, G, stride=L), :] = hf
        hb = ab_s[pl.ds(tb, G, stride=L), :] * hb + ub_s[pl.ds(tb, G, stride=L), :]
        hb_s[pl.ds(tb, G, stride=L), :] = hb
        return hf, hb

    hf, hb = lax.fori_loop(0, L, step, (h0_ref[0], h0_ref[1]), unroll=8)
    st_ref[0] = hf
    st_ref[1] = hb

    def fin(b, carry):
        r0 = pl.multiple_of(b * L, L)
        o_ref[pl.ds(r0, L), :] = (hf_s[pl.ds(r0, L), :] + hb_s[pl.ds(r0, L), :]) * _gelu_tanh(g_ref[pl.ds(r0, L), :])
        return carry

    lax.fori_loop(0, G, fin, 0)


def _lru(p, h0, conv_w, conv_b, wg, bg, lam, *, row0, L, B, G):
    nc = LRU_W // LCB
    xcol = (ATT_W + 2 * KV_W) // LCB
    gcol = xcol + nc
    rb0 = row0 // (G * L)
    kern = functools.partial(_lru_kernel, L=L, G=G)
    return pl.pallas_call(
        kern,
        grid=(B // G, nc),
        in_specs=[
            pl.BlockSpec((G * L, LCB), lambda i, c: (rb0 + i, xcol + c)),
            pl.BlockSpec((G * L, LCB), lambda i, c: (rb0 + i, gcol + c)),
            pl.BlockSpec((2, G, LCB), lambda i, c: (0, i, c)),
            pl.BlockSpec((LRU_CONV, LCB), lambda i, c: (0, c)),
            pl.BlockSpec((1, LCB), lambda i, c: (0, c)),
            pl.BlockSpec((None, LCB, 4 * LCB), lambda i, c: (c, 0, 0)),
            pl.BlockSpec((None, 1, 4 * LCB), lambda i, c: (c, 0, 0)),
            pl.BlockSpec((2, LCB), lambda i, c: (0, c)),
        ],
        out_specs=[
            pl.BlockSpec((G * L, LCB), lambda i, c: (i, c)),
            pl.BlockSpec((2, G, LCB), lambda i, c: (0, i, c)),
        ],
        out_shape=[jax.ShapeDtypeStruct((B * L, LRU_W), F32),
                   jax.ShapeDtypeStruct((2, B, LRU_W), F32)],
        scratch_shapes=[pltpu.VMEM((G * L, LCB), F32) for _ in range(6)],
        compiler_params=_params("arbitrary", "arbitrary"),
        name="lru",
    )(p, p, h0, conv_w, conv_b.reshape(1, LRU_W), wg, bg, lam)


LRU_TC = 32


def _lru_tm_kernel(xr_ref, g_ref, h0_ref, cw_ref, cb_ref, wg_ref, bg_ref, lam_ref, o_ref, st_ref,
                   xp_s, af_s, uf_s, ab_s, ub_s, hf_s, hb_s, *, L, G):
    sp = jnp.maximum(-lam_ref[...], 0.0) + jnp.log1p(jnp.exp(-jnp.abs(lam_ref[...])))
    nsp = -LRU_C * sp
    left = LRU_CONV // 2
    right = LRU_CONV - 1 - left
    xp_s[0:left] = jnp.zeros((left, G, LCB), F32)
    xp_s[left + L:left + L + right] = jnp.zeros((right, G, LCB), F32)
    xp_s[left:left + L] = jnp.swapaxes(xr_ref[...].reshape(G, L, LCB), 0, 1)

    def prep(c, carry):
        t0 = pl.multiple_of(c * LRU_TC, LRU_TC)
        xc = cb_ref[...].reshape(1, 1, LCB)
        for k in range(LRU_CONV):
            xc = xc + xp_s[pl.ds(t0 + k, LRU_TC)] * cw_ref[k:k + 1, :].reshape(1, 1, LCB)
        xc = xc.reshape(LRU_TC * G, LCB)
        gates = _dot(xc.astype(BF16), wg_ref[...]) + bg_ref[...]
        for d, (a_s, u_s) in enumerate(((af_s, uf_s), (ab_s, ub_s))):
            r = _sigmoid(gates[:, (2 * d) * LCB:(2 * d + 1) * LCB])
            gi = _sigmoid(gates[:, (2 * d + 1) * LCB:(2 * d + 2) * LCB])
            log_a = r * nsp[d:d + 1, :]
            a = jnp.exp(log_a)
            a_s[pl.ds(t0, LRU_TC)] = a.reshape(LRU_TC, G, LCB)
            u = _sqrt_nonneg(jnp.tanh(-log_a) * (1.0 + a * a)) * (gi * xc)
            u_s[pl.ds(t0, LRU_TC)] = u.reshape(LRU_TC, G, LCB)
        return carry

    lax.fori_loop(0, L // LRU_TC, prep, 0)

    def step(t, carry):
        hf, hb = carry
        tb = L - 1 - t
        hf = af_s[t] * hf + uf_s[t]
        hf_s[t] = hf
        hb = ab_s[tb] * hb + ub_s[tb]
        hb_s[tb] = hb
        return hf, hb

    hf, hb = lax.fori_loop(0, L, step, (h0_ref[0], h0_ref[1]), unroll=8)
    st_ref[0] = hf
    st_ref[1] = hb
    h = jnp.swapaxes(hf_s[...] + hb_s[...], 0, 1).reshape(G * L, LCB)
    o_ref[...] = h * _gelu_tanh(g_ref[...])


def _lru_tm(p, h0, conv_w, conv_b, wg, bg, lam, *, row0, L, B, G):
    nc = LRU_W // LCB
    xcol = (ATT_W + 2 * KV_W) // LCB
    gcol = xcol + nc
    rb0 = row0 // (G * L)
    tm = lambda extra: pltpu.VMEM((L + extra, G, LCB), F32)
    return pl.pallas_call(
        functools.partial(_lru_tm_kernel, L=L, G=G),
        grid=(B // G, nc),
        in_specs=[
            pl.BlockSpec((G * L, LCB), lambda i, c: (rb0 + i, xcol + c)),
            pl.BlockSpec((G * L, LCB), lambda i, c: (rb0 + i, gcol + c)),
            pl.BlockSpec((2, G, LCB), lambda i, c: (0, i, c)),
            pl.BlockSpec((LRU_CONV, LCB), lambda i, c: (0, c)),
            pl.BlockSpec((1, LCB), lambda i, c: (0, c)),
            pl.BlockSpec((None, LCB, 4 * LCB), lambda i, c: (c, 0, 0)),
            pl.BlockSpec((None, 1, 4 * LCB), lambda i, c: (c, 0, 0)),
            pl.BlockSpec((2, LCB), lambda i, c: (0, c)),
        ],
        out_specs=[
            pl.BlockSpec((G * L, LCB), lambda i, c: (i, c)),
            pl.BlockSpec((2, G, LCB), lambda i, c: (0, i, c)),
        ],
        out_shape=[jax.ShapeDtypeStruct((B * L, LRU_W), F32),
                   jax.ShapeDtypeStruct((2, B, LRU_W), F32)],
        scratch_shapes=[tm(LRU_CONV - 1)] + [tm(0) for _ in range(6)],
        compiler_params=_params("arbitrary", "arbitrary"),
        name="lru_tm",
    )(p, p, h0, conv_w, conv_b.reshape(1, LRU_W), wg, bg, lam)


def _lru_gate_weights(w_r, b_r, w_i, b_i):
    nc = LRU_W // LCB
    per = LCB // LRU_BD
    eye = jnp.eye(per, dtype=F32)

    def dense(w):
        w = w.reshape(nc, per, LRU_BD, LRU_BD)
        d = jnp.einsum('chij,hg->chigj', w, eye)
        return d.reshape(nc, LCB, LCB)

    cols = [dense(w_r[0]), dense(w_i[0]), dense(w_r[1]), dense(w_i[1])]
    wg = jnp.concatenate(cols, axis=-1).astype(BF16)
    bias = [b.reshape(nc, 1, LCB) for b in (b_r[0], b_i[0], b_r[1], b_i[1])]
    return wg, jnp.concatenate(bias, axis=-1)


def _fnet_kernel(f_ref, seq_ref, grp_ref, o_ref, cs_s, *, L, G):
    grp = grp_ref[...]
    for b in range(G):
        for gi in range(FNET_GROUPS):
            f = f_ref[b * L:(b + 1) * L, gi * FNET_GD:(gi + 1) * FNET_GD].astype(BF16)
            cs = _dot(f, grp).astype(BF16)
            col = b * FNET_W + gi * FNET_GD
            cs_s[0:L, col:col + FNET_GD] = cs[:, :FNET_GD]
            cs_s[L:2 * L, col:col + FNET_GD] = cs[:, FNET_GD:]
    y = _dot(seq_ref[...], cs_s[...])
    for b in range(G):
        o_ref[b * L:(b + 1) * L, :] = y[:, b * FNET_W:(b + 1) * FNET_W]


def _fnet(p, *, row0, L, B, G):
    seq, grp = _fnet_tables(L)
    rb0 = row0 // (G * L)
    return pl.pallas_call(
        functools.partial(_fnet_kernel, L=L, G=G),
        grid=(B // G,),
        in_specs=[
            pl.BlockSpec((G * L, FNET_W), lambda b: (rb0 + b, 0)),
            _resident((L, 2 * L), lambda b: (0, 0)),
            _resident((FNET_GD, 2 * FNET_GD), lambda b: (0, 0)),
        ],
        out_specs=pl.BlockSpec((G * L, FNET_W), lambda b: (b, 0)),
        out_shape=jax.ShapeDtypeStruct((B * L, FNET_W), F32),
        scratch_shapes=[pltpu.VMEM((2 * L, G * FNET_W), BF16)],
        compiler_params=_params("arbitrary"),
        name="fnet",
    )(p, _table_bf16(seq), _table_bf16(grp))


def _alternating_sign(row):
    return (1 - 2 * (row & 1)).astype(F32)


def _filter_kernel(zh_ref, zl_ref, w1_ref, b1_ref, w2_ref, b2_ref, w3f_ref, w3b_ref, fr_ref, ldf_ref, ldb_ref,
                   fc_ref, fs_ref, fcb_ref, fsb_ref, k1_ref, k2_ref, kn_ref, hid_hi_s, hid_lo_s, *, L):
    @pl.when((pl.program_id(0) == 0) & (pl.program_id(1) == 0))
    def _():
        hid = jnp.sin(fr_ref[0:1, :] * (_dot3((zh_ref[...], zl_ref[...]), _split(w1_ref[...])) + b1_ref[...]))
        hid = jnp.sin(fr_ref[1:2, :] * (_dot3(_split(hid), _split(w2_ref[...])) + b2_ref[...]))
        hid_hi_s[...], hid_lo_s[...] = _split(hid)

    hid = (hid_hi_s[...], hid_lo_s[...])
    tn = lax.broadcasted_iota(jnp.int32, (L, CB), 0).astype(F32) * (1.0 / L)
    ff = _dot3(hid, _split(w3f_ref[...])) * jnp.exp(-tn * jnp.exp(ldf_ref[...]))
    fb = _dot3(hid, _split(w3b_ref[...])) * jnp.exp(-tn * jnp.exp(ldb_ref[...]))
    ss = jnp.sum(ff * ff, axis=0, keepdims=True) + jnp.sum(fb * fb, axis=0, keepdims=True)
    scale = lax.rsqrt(ss + EPS)
    row = lax.broadcasted_iota(jnp.int32, (L, CB), 0)
    sgn = _alternating_sign(row)
    nyq = jnp.sum(sgn * ff, axis=0, keepdims=True) - jnp.sum(sgn * fb, axis=0, keepdims=True)
    ff = ff.astype(BF16)
    fb = fb.astype(BF16)
    kr = _dot(fc_ref[...], ff) + _dot(fcb_ref[...], fb)
    ki = _dot(fs_ref[...], ff) + _dot(fsb_ref[...], fb)
    w = jnp.where(row == 0, 0.5 / L, 1.0 / L) * scale
    k1_ref[0] = kr * w
    k2_ref[0] = ki * w
    kn_ref[0] = nyq * (scale * (0.5 / L))


def _hyena_filters(L, w1, b1, w2, b2, w3, freq, log_decay):
    zh, zl = _hyena_embedding(L)
    tabs = [_table_bf16(t) for t in _rdft_tables(L)]
    nc = HY_W // CB
    w1p = jnp.zeros((EMB_PAD, HY_HID), F32).at[:HY_EMB].set(w1)
    bcol = HY_ORDER * nc
    tab_spec = _resident((L, L), lambda n, c: (0, 0))
    small = lambda shape: _resident(shape, lambda n, c: (0, 0))
    return pl.pallas_call(
        functools.partial(_filter_kernel, L=L),
        grid=(HY_ORDER, nc),
        in_specs=[
            small((L, EMB_PAD)), small((L, EMB_PAD)),
            small((EMB_PAD, HY_HID)), small((1, HY_HID)),
            small((HY_HID, HY_HID)), small((1, HY_HID)),
            pl.BlockSpec((HY_HID, CB), lambda n, c: (0, n * nc + c)),
            pl.BlockSpec((HY_HID, CB), lambda n, c: (0, bcol + n * nc + c)),
            small((2, HY_HID)),
            pl.BlockSpec((1, CB), lambda n, c: (0, n * nc + c)),
            pl.BlockSpec((1, CB), lambda n, c: (0, bcol + n * nc + c)),
        ] + [tab_spec] * 4,
        out_specs=[
            pl.BlockSpec((1, L, CB), lambda n, c: (n, 0, c)),
            pl.BlockSpec((1, L, CB), lambda n, c: (n, 0, c)),
            pl.BlockSpec((1, 1, CB), lambda n, c: (n, 0, c)),
        ],
        out_shape=[jax.ShapeDtypeStruct((HY_ORDER, L, HY_W), F32),
                   jax.ShapeDtypeStruct((HY_ORDER, L, HY_W), F32),
                   jax.ShapeDtypeStruct((HY_ORDER, 1, HY_W), F32)],
        scratch_shapes=[pltpu.VMEM((L, HY_HID), BF16), pltpu.VMEM((L, HY_HID), BF16)],
        compiler_params=_params("arbitrary", "arbitrary"),
        name="hyena_filter",
    )(zh, zl, w1p, b1.reshape(1, HY_HID), w2, b2.reshape(1, HY_HID), w3, w3, freq,
      log_decay.reshape(1, HY_FILT), log_decay.reshape(1, HY_FILT), *tabs)


def _hyena_kernel(v_ref, x1_ref, x2_ref, cwv_ref, cw1_ref, cw2_ref, cbv_ref, cb1_ref, cb2_ref, bias_ref,
                  k1_ref, k2_ref, kn_ref, fc_ref, fs_ref, o_ref, *, L, G):
    row = lax.broadcasted_iota(jnp.int32, (L, CB), 0)
    left = HY_CONV // 2

    def conv(x_ref, cw_ref, cb_ref):
        outs = []
        for g in range(G):
            x = x_ref[g * L:(g + 1) * L, :]
            y = cb_ref[...]
            for k in range(HY_CONV):
                y = y + _shift_rows(x, row, left - k) * cw_ref[k:k + 1, :]
            outs.append(y)
        return outs[0] if G == 1 else jnp.concatenate(outs, axis=1)

    def per_seq(x):
        return x if G == 1 else jnp.concatenate([x] * G, axis=1)

    fc = fc_ref[...]
    fs = fs_ref[...]
    sgn = per_seq(_alternating_sign(row))
    z = conv(v_ref, cwv_ref, cbv_ref)
    gates = (conv(x1_ref, cw1_ref, cb1_ref), conv(x2_ref, cw2_ref, cb2_ref))
    for n in range(HY_ORDER):
        zb = z.astype(BF16)
        a = _dot(fc, zb)
        b = _dot(fs, zb)
        nyq = jnp.sum(sgn * z, axis=0, keepdims=True)
        k1 = per_seq(k1_ref[n])
        k2 = per_seq(k2_ref[n])
        p = a * k1 - b * k2
        q = a * k2 + b * k1
        y = _dot(fc, p.astype(BF16)) + _dot(fs, q.astype(BF16)) + sgn * (nyq * per_seq(kn_ref[n]))
        z = gates[n] * (y + per_seq(bias_ref[n:n + 1, :]) * z)
    for g in range(G):
        o_ref[g * L:(g + 1) * L, :] = z[:, g * CB:(g + 1) * CB]


def _hyena(p, conv_w, conv_b, hy_bias, k1, k2, kn, *, row0, L, B, G):
    fc, fs = (_table_bf16(t) for t in _rdft_tables(L)[:2])
    nc = HY_W // CB
    rb0 = row0 // (G * L)
    c0 = FNET_W // CB
    conv_b = conv_b.reshape(1, (HY_ORDER + 1) * HY_W)
    x_spec = lambda j: pl.BlockSpec((G * L, CB), lambda c, b: (rb0 + b, c0 + j * nc + c))
    cw_spec = lambda j: pl.BlockSpec((HY_CONV, CB), lambda c, b: (0, j * nc + c))
    cb_spec = lambda j: pl.BlockSpec((1, CB), lambda c, b: (0, j * nc + c))
    k_spec = pl.BlockSpec((HY_ORDER, L, CB), lambda c, b: (0, 0, c))
    return pl.pallas_call(
        functools.partial(_hyena_kernel, L=L, G=G),
        grid=(nc, B // G),
        in_specs=[x_spec(0), x_spec(1), x_spec(2), cw_spec(0), cw_spec(1), cw_spec(2),
                  cb_spec(0), cb_spec(1), cb_spec(2),
                  pl.BlockSpec((HY_ORDER, CB), lambda c, b: (0, c)),
                  k_spec, k_spec, pl.BlockSpec((HY_ORDER, 1, CB), lambda c, b: (0, 0, c))]
                 + [_resident((L, L), lambda c, b: (0, 0))] * 2,
        out_specs=pl.BlockSpec((G * L, CB), lambda c, b: (b, c)),
        out_shape=jax.ShapeDtypeStruct((B * L, HY_W), F32),
        compiler_params=_params("arbitrary", "arbitrary"),
        name="hyena",
    )(p, p, p, conv_w, conv_w, conv_w, conv_b, conv_b, conv_b, hy_bias, k1, k2, kn, fc, fs)


def kernel(x_prompt, x_sample, c, cache_k, cache_v, state_lru, c_ctx, mod_w, mod_b, norm_mix, norm_mlp, norm_final, mlp_w1, mlp_w2, ev_w_in, ev_w_out, attn_sink, lru_conv_w, lru_conv_b, lru_w_r, lru_b_r, lru_w_i, lru_b_i, lru_lambda, od_w_in, od_w_out, hy_conv_w, hy_conv_b, hy_w1, hy_b1, hy_w2, hy_b2, hy_w3, hy_freq, hy_log_decay, hy_bias):
    x = (x_prompt.reshape(N_CTX, D_MODEL), x_sample.reshape(N_LAT, D_MODEL))
    cvec = jnp.concatenate([c_ctx[None, :], c, jnp.zeros((8 - N_GROUPS, D_MODEL), F32)], axis=0)
    mods = _modulation(cvec, mod_w, mod_b)
    zero_state = jnp.zeros((2, BATCH, LRU_W), F32)
    ev_w_in_b, ev_w_out_b = ev_w_in.astype(BF16), ev_w_out.astype(BF16)
    od_w_in_b, od_w_out_b = od_w_in.astype(BF16), od_w_out.astype(BF16)
    w1_b, w2_b = mlp_w1[0].astype(BF16), mlp_w2[0].astype(BF16)
    k_list, v_list, s_list = [], [], []
    h_next = None
    for l in range(DEPTH):
        j = l // 2
        w_in = ev_w_in_b if l % 2 == 0 else od_w_in_b
        p = _pre(x, norm_mix[l], mods[l], w_in, j) if l == 0 else _project(h_next, w_in, j)
        if l % 2 == 0:
            att_ctx, k_ctx, v_ctx = _attn_ctx(p, attn_sink[j])
            k_list.append(k_ctx.reshape(BATCH, SEQ, N_KV, HEAD_DIM))
            v_list.append(v_ctx.reshape(BATCH, SEQ, N_KV, HEAD_DIM))
            att_lat = _attn_lat(p, attn_sink[j], cache_k[:, j].reshape(DEC_BATCH, PAST_LEN, KV_W),
                                cache_v[:, j].reshape(DEC_BATCH, PAST_LEN, KV_W))
            wg, bg = _lru_gate_weights(lru_w_r[j], lru_b_r[j], lru_w_i[j], lru_b_i[j])
            lru_args = (lru_conv_w[j], lru_conv_b[j], wg, bg, lru_lambda[j])
            y_ctx, st = _lru_tm(p, zero_state, *lru_args, row0=0, L=SEQ, B=BATCH, G=8)
            h0_lat = jnp.swapaxes(state_lru[:, j], 0, 1)
            y_lat, _ = _lru(p, h0_lat, *lru_args, row0=N_CTX, L=DEC_SEQ, B=DEC_BATCH, G=DEC_BATCH)
            s_list.append(jnp.swapaxes(st, 0, 1))
            m1, m2 = (att_ctx, att_lat), (y_ctx, y_lat)
            w_out = ev_w_out_b
        else:
            filt =(hy_w1[j], hy_b1[j], hy_w2[j], hy_b2[j], hy_w3[j], hy_freq[j], hy_log_decay[j])
            m1 = (_fnet(p, row0=0, L=SEQ, B=BATCH, G=4), _fnet(p, row0=N_CTX, L=DEC_SEQ, B=DEC_BATCH, G=1))
            hy = []
            for row0, L, B, G in ((0, SEQ, BATCH, 4), (N_CTX, DEC_SEQ, DEC_BATCH, 1)):
                k1, k2, kn = _hyena_filters(L, *filt)
                hy.append(_hyena(p, hy_conv_w[j], hy_conv_b[j], hy_bias[j], k1, k2, kn, row0=row0, L=L, B=B, G=G))
            m2 = tuple(hy)
            w_out = od_w_out_b
        last = l == DEPTH - 1
        post = functools.partial(_post, x, m1, m2, mods[l], mods[l if last else l + 1], norm_mlp[l],
                                 norm_final if last else norm_mix[l + 1], w_out, w1_b, w2_b, mlp_w1, mlp_w2, j, l)
        if not last:
            x, h_next, w1_b, w2_b = post(final=False)
        else:
            y_prompt = post(final=True, tile0=0, n_tiles=N_CTX_TILES).reshape(BATCH, SEQ, D_MODEL)
            y_sample = post(final=True, tile0=N_CTX_TILES, n_tiles=N_LAT // TM).reshape(DEC_BATCH, DEC_SEQ, D_MODEL)
    k_state = jnp.stack(k_list, axis=1)
    v_state = jnp.stack(v_list, axis=1)
    lru_state = jnp.stack(s_list, axis=1).astype(x_prompt.dtype)
    return (y_prompt, y_sample, k_state, v_state, lru_state)
```

```python
import functools
import math

import numpy as np
import jax
import jax.numpy as jnp
from jax import lax
from jax.experimental import pallas as pl
from jax.experimental.pallas import tpu as pltpu

F32 = jnp.float32
BF16 = jnp.bfloat16

D_MODEL = 1024
BATCH = 32
SEQ = 256
DEPTH = 4
DEC_BATCH = 2
DEC_SEQ = 1024
PAST_LEN = 256
GRID_W = 64
N_HEADS = 8
N_KV = 2
HEAD_DIM = 64
GQA_G = N_HEADS // N_KV
ATT_W = N_HEADS * HEAD_DIM
KV_W = N_KV * HEAD_DIM
WINDOW = 128
BLOCK = 128
ROPE_BASE = 10000.0
LRU_W = 512
LRU_BLOCKS = 8
LRU_BD = LRU_W // LRU_BLOCKS
LRU_CONV = 4
LRU_C = 8.0
FNET_GROUPS = 4
FNET_W = 512
FNET_GD = FNET_W // FNET_GROUPS
HY_W = 512
HY_ORDER = 2
HY_CONV = 3
HY_BANDS = 16
HY_EMB = 1 + 2 * HY_BANDS
HY_HID = 64
HY_FILT = 2 * HY_ORDER * HY_W
EVEN_IN = ATT_W + 2 * KV_W + 2 * LRU_W
ODD_IN = FNET_W + (HY_ORDER + 1) * HY_W
D_FF = 4 * D_MODEL
N_MOD = 6
EPS = 1e-6
NEG = -1e30

N_CTX = BATCH * SEQ
N_LAT = DEC_BATCH * DEC_SEQ
N_TOK = N_CTX + N_LAT
N_GROUPS = 1 + DEC_BATCH

VMEM_LIMIT = 56 * 1024 * 1024
TM = 512
CB = 256
LCB = 128
FF_CHUNK = 1024
EMB_PAD = 128


def _params(*sem):
    return pltpu.CompilerParams(dimension_semantics=sem, vmem_limit_bytes=VMEM_LIMIT)


def _resident(shape, index_map):
    return pl.BlockSpec(shape, index_map, pipeline_mode=pl.Buffered(1))


def _dot(a, b):
    return jnp.dot(a, b, preferred_element_type=F32)


def _dot_nt(a, b):
    return lax.dot_general(a, b, (((1,), (1,)), ((), ())), preferred_element_type=F32)


def _split(x):
    hi = x.astype(BF16)
    lo = (x - hi.astype(F32)).astype(BF16)
    return hi, lo


def _dot3(a, b):
    return _dot(a[0], b[0]) + (_dot(a[1], b[0]) + _dot(a[0], b[1]))


def _np_f32(x):
    return np.asarray(x, dtype=np.float32)


def _table_bf16(t):
    return jnp.asarray(t).astype(BF16)


def _np_split(x):
    hi = np.asarray(x, dtype=np.float32).astype(BF16)
    lo = (np.asarray(x, dtype=np.float64) - hi.astype(np.float64)).astype(np.float32).astype(BF16)
    return hi, lo


def _group_of_tile(i):
    return jnp.maximum(i * TM - (N_CTX - DEC_SEQ), 0) // DEC_SEQ


@functools.lru_cache(maxsize=None)
def _rdft_tables(L):
    k = np.arange(L, dtype=np.int64)[:, None]
    t = np.arange(L, dtype=np.int64)[None, :]
    ang = (k * t % (2 * L)).astype(np.float64) * (math.pi / L)
    tb = 2 * L - 1 - t
    angb = (k * tb % (2 * L)).astype(np.float64) * (math.pi / L)
    return tuple(_np_f32(a) for a in (np.cos(ang), -np.sin(ang), np.cos(angb), -np.sin(angb)))


@functools.lru_cache(maxsize=None)
def _fnet_tables(L):
    k = np.arange(L, dtype=np.int64)[:, None]
    t = np.arange(L, dtype=np.int64)[None, :]
    ang = (k * t % L).astype(np.float64) * (2.0 * math.pi / L)
    scale = 1.0 / math.sqrt(L * FNET_GD)
    seq = np.concatenate([np.cos(ang), -np.sin(ang)], axis=1) * scale
    m = np.arange(FNET_GD, dtype=np.int64)[:, None]
    d = np.arange(FNET_GD, dtype=np.int64)[None, :]
    angg = (m * d % FNET_GD).astype(np.float64) * (2.0 * math.pi / FNET_GD)
    grp = np.concatenate([np.cos(angg), np.sin(angg)], axis=1)
    return _np_f32(seq), _np_f32(grp)


@functools.lru_cache(maxsize=None)
def _rope_tables():
    L = DEC_SEQ
    n = HEAD_DIM // 4
    row = np.repeat(np.arange(L // GRID_W), GRID_W).astype(np.float32)
    col = np.tile(np.arange(GRID_W), L // GRID_W).astype(np.float32)
    inv = (np.float32(ROPE_BASE) ** (-np.arange(n, dtype=np.float32) / np.float32(n))).astype(np.float32)
    ar = (row[:, None] * inv[None, :]).astype(np.float64)
    ac = (col[:, None] * inv[None, :]).astype(np.float64)
    cos = np.concatenate([np.cos(ar), np.cos(ar), np.cos(ac), np.cos(ac)], axis=1)
    sin = np.concatenate([-np.sin(ar), np.sin(ar), -np.sin(ac), np.sin(ac)], axis=1)
    cos = np.tile(cos, (1, 2)).astype(np.float32)
    sin = np.tile(sin, (1, 2)).astype(np.float32)
    return cos, sin


@functools.lru_cache(maxsize=None)
def _hyena_embedding(L):
    t = np.arange(L, dtype=np.float32)
    tn = (t / np.float32(L)).astype(np.float64)
    bands = np.linspace(1e-4, HY_BANDS - 1, HY_BANDS, dtype=np.float32).astype(np.float64)
    w = (np.float32(2.0 * math.pi / L) * t).astype(np.float64)
    z = np.concatenate([tn[:, None], np.cos(w[:, None] * bands), -np.sin(w[:, None] * bands)], axis=-1)
    zp = np.zeros((L, EMB_PAD), np.float64)
    zp[:, :HY_EMB] = z
    return _np_split(zp)


MOD_TN = 1536


def _mod_kernel(c_ref, w_ref, b_ref, o_ref):
    c = c_ref[...]
    s_hi, s_lo = _split(c * jax.nn.sigmoid(c))
    r = _dot(jnp.concatenate([s_hi, s_lo], axis=0), w_ref[0].astype(BF16))
    o_ref[0] = r[:8] + r[8:] + b_ref[0]


def _modulation(cvec, mod_w, mod_b):
    out = pl.pallas_call(
        _mod_kernel,
        grid=(DEPTH, N_MOD * D_MODEL // MOD_TN),
        in_specs=[
            pl.BlockSpec((8, D_MODEL), lambda l, n: (0, 0)),
            pl.BlockSpec((1, D_MODEL, MOD_TN), lambda l, n: (l, 0, n)),
            pl.BlockSpec((1, 1, MOD_TN), lambda l, n: (l, 0, n)),
        ],
        out_specs=pl.BlockSpec((1, 8, MOD_TN), lambda l, n: (l, 0, n)),
        out_shape=jax.ShapeDtypeStruct((DEPTH, 8, N_MOD * D_MODEL), F32),
        compiler_params=_params("arbitrary", "arbitrary"),
        name="modulation",
    )(cvec, mod_w, mod_b.reshape(DEPTH, 1, N_MOD * D_MODEL))
    return out.reshape(DEPTH, 8, N_MOD, D_MODEL)[:, :N_GROUPS]


def _rms(x, g):
    return x * lax.rsqrt(jnp.mean(x * x, axis=-1, keepdims=True) + EPS) * g


N_CTX_TILES = N_CTX // TM


def _row_stream(x, tile0):
    if not isinstance(x, tuple):
        return [x], [pl.BlockSpec((TM, x.shape[1]), lambda i: (i + tile0, 0))]
    ctx, lat = x
    return [ctx, lat], [
        pl.BlockSpec((TM, ctx.shape[1]), lambda i: (jnp.minimum(i + tile0, N_CTX_TILES - 1), 0)),
        pl.BlockSpec((TM, lat.shape[1]), lambda i: (jnp.maximum(i + tile0 - N_CTX_TILES, 0), 0)),
    ]


def _read_rows(refs, paired, tile):
    vals = []
    k = 0
    for p in paired:
        if p:
            vals.append(jnp.where(tile < N_CTX_TILES, refs[k][...], refs[k + 1][...]))
            k += 2
        else:
            vals.append(refs[k][...])
            k += 1
    return vals, refs[k:]


def _pre_kernel(*refs, paired, tile0):
    (x,), (g_ref, mod_ref, w_ref, o_ref) = _read_rows(refs, paired, pl.program_id(0) + tile0)
    h = _rms(x, g_ref[...]) * (1.0 + mod_ref[1:2, :]) + mod_ref[0:1, :]
    o_ref[...] = _dot(h.astype(BF16), w_ref[...])


def _pre(x, g, mod, w_in, j):
    n_in = w_in.shape[2]
    ops, specs = _row_stream(x, 0)
    return pl.pallas_call(
        functools.partial(_pre_kernel, paired=(isinstance(x, tuple),), tile0=0),
        grid=(N_TOK // TM,),
        in_specs=specs + [
            _resident((1, D_MODEL), lambda i: (0, 0)),
            pl.BlockSpec((None, N_MOD, D_MODEL), lambda i: (_group_of_tile(i), 0, 0)),
            _resident((None, D_MODEL, n_in), lambda i: (j, 0, 0)),
        ],
        out_specs=pl.BlockSpec((TM, n_in), lambda i: (i, 0)),
        out_shape=jax.ShapeDtypeStruct((N_TOK, n_in), F32),
        compiler_params=_params("arbitrary"),
        name="pre",
    )(*ops, g.reshape(1, D_MODEL), mod, w_in)


def _post_kernel(*refs, paired, tile0, final):
    (x, m1, m2), rest = _read_rows(refs, paired, pl.program_id(0) + tile0)
    mod_ref, modn_ref, g_ref, gf_ref, wo_ref, w1_ref, w2_ref = rest[:7]
    if final:
        (o_ref,) = rest[7:]
    else:
        w1f_ref, w2f_ref, o_ref, hn_ref, w1n_ref, w2n_ref = rest[7:]
    half = wo_ref.shape[0] // 2
    o = _dot(m1.astype(BF16), wo_ref[:half, :]) + _dot(m2.astype(BF16), wo_ref[half:, :])
    x1 = x + mod_ref[2:3, :] * o
    h = (_rms(x1, g_ref[...]) * (1.0 + mod_ref[4:5, :]) + mod_ref[3:4, :]).astype(BF16)
    acc = jnp.zeros((TM, D_MODEL), F32)
    for c in range(D_FF // FF_CHUNK):
        t = jnp.maximum(_dot(h, w1_ref[:, c * FF_CHUNK:(c + 1) * FF_CHUNK]), 0.0)
        acc = acc + _dot((t * t).astype(BF16), w2_ref[c * FF_CHUNK:(c + 1) * FF_CHUNK, :])
    x2 = x1 + mod_ref[5:6, :] * acc
    if final:
        o_ref[...] = _rms(x2, gf_ref[...])
    else:
        o_ref[...] = x2
        hn_ref[...] = (_rms(x2, gf_ref[...]) * (1.0 + modn_ref[1:2, :]) + modn_ref[0:1, :]).astype(BF16)
        w1n_ref[...] = w1f_ref[...].astype(BF16)
        w2n_ref[...] = w2f_ref[...].astype(BF16)


W_CHUNKS = 16


def _post(x, m1, m2, mod, mod_next, g, g_tail, w_out, w1, w2, w1_f32, w2_f32, j, l, *, final, tile0=0,
          n_tiles=N_TOK // TM):
    mod_spec = pl.BlockSpec((None, N_MOD, D_MODEL), lambda i: (_group_of_tile(i + tile0), 0, 0))
    row_spec = pl.BlockSpec((TM, D_MODEL), lambda i: (i, 0))
    x_shape = jax.ShapeDtypeStruct((n_tiles * TM, D_MODEL), F32)
    extra_in, extra_ops, extra_out, extra_shapes = [], [], [], []
    if not final:
        assert n_tiles >= W_CHUNKS
        cw = D_FF // W_CHUNKS
        chunk = lambda i: jnp.minimum(i, W_CHUNKS - 1)
        extra_in = [pl.BlockSpec((None, D_MODEL, cw), lambda i: (l + 1, 0, chunk(i))),
                    pl.BlockSpec((None, cw, D_MODEL), lambda i: (l + 1, chunk(i), 0))]
        extra_ops = [w1_f32, w2_f32]
        extra_out = [row_spec, pl.BlockSpec((D_MODEL, cw), lambda i: (0, chunk(i))),
                     pl.BlockSpec((cw, D_MODEL), lambda i: (chunk(i), 0))]
        extra_shapes = [jax.ShapeDtypeStruct((n_tiles * TM, D_MODEL), BF16),
                        jax.ShapeDtypeStruct((D_MODEL, D_FF), BF16), jax.ShapeDtypeStruct((D_FF, D_MODEL), BF16)]
    ops, specs, paired = [], [], []
    for s in (x, m1, m2):
        o, sp = _row_stream(s, tile0)
        ops += o
        specs += sp
        paired.append(isinstance(s, tuple))
    return pl.pallas_call(
        functools.partial(_post_kernel, paired=tuple(paired), tile0=tile0, final=final),
        grid=(n_tiles,),
        in_specs=specs + [
            mod_spec,
            mod_spec,
            _resident((1, D_MODEL), lambda i: (0, 0)),
            _resident((1, D_MODEL), lambda i: (0, 0)),
            _resident((None,) + w_out.shape[1:], lambda i: (j, 0, 0)),
            _resident(w1.shape, lambda i: (0, 0)),
            _resident(w2.shape, lambda i: (0, 0)),
        ] + extra_in,
        out_specs=row_spec if final else [row_spec] + extra_out,
        out_shape=x_shape if final else [x_shape] + extra_shapes,
        compiler_params=_params("arbitrary"),
        name="post",
    )(*ops, mod, mod_next, g.reshape(1, D_MODEL), g_tail.reshape(1, D_MODEL), w_out, w1, w2, *extra_ops)


def _project_kernel(h_ref, w_ref, o_ref):
    o_ref[...] = _dot(h_ref[...], w_ref[...])


def _project(h, w_in, j):
    n_in = w_in.shape[2]
    return pl.pallas_call(
        _project_kernel,
        grid=(N_TOK // TM,),
        in_specs=[pl.BlockSpec((TM, D_MODEL), lambda i: (i, 0)),
                  _resident((None, D_MODEL, n_in), lambda i: (j, 0, 0))],
        out_specs=pl.BlockSpec((TM, n_in), lambda i: (i, 0)),
        out_shape=jax.ShapeDtypeStruct((N_TOK, n_in), F32),
        compiler_params=_params("arbitrary"),
        name="project",
    )(h, w_in)


SCALE = HEAD_DIM ** -0.5


def _attn_ctx_kernel(sink_ref, q_ref, k_ref, v_ref, o_ref, ko_ref, vo_ref):
    ko_ref[...] = k_ref[...]
    vo_ref[...] = v_ref[...]
    k = k_ref[...].astype(BF16)
    v = v_ref[...].astype(BF16)
    for hd in range(N_HEADS):
        kv = hd // GQA_G
        sk = sink_ref[hd]
        qh = q_ref[:, hd * HEAD_DIM:(hd + 1) * HEAD_DIM].astype(BF16)
        s = _dot_nt(qh, k[:, kv * HEAD_DIM:(kv + 1) * HEAD_DIM]) * SCALE
        m = jnp.maximum(jnp.max(s, axis=-1, keepdims=True), sk)
        e = jnp.exp(s - m)
        den = jnp.sum(e, axis=-1, keepdims=True) + jnp.exp(sk - m)
        o = _dot(e.astype(BF16), v[:, kv * HEAD_DIM:(kv + 1) * HEAD_DIM])
        o_ref[:, hd * HEAD_DIM:(hd + 1) * HEAD_DIM] = o / den


def _attn_ctx(p, sink):
    kcol = ATT_W // KV_W
    return pl.pallas_call(
        _attn_ctx_kernel,
        grid=(BATCH,),
        in_specs=[
            pl.BlockSpec(memory_space=pltpu.SMEM),
            pl.BlockSpec((SEQ, ATT_W), lambda b: (b, 0)),
            pl.BlockSpec((SEQ, KV_W), lambda b: (b, kcol)),
            pl.BlockSpec((SEQ, KV_W), lambda b: (b, kcol + 1)),
        ],
        out_specs=[pl.BlockSpec((SEQ, ATT_W), lambda b: (b, 0)),
                   pl.BlockSpec((SEQ, KV_W), lambda b: (b, 0)),
                   pl.BlockSpec((SEQ, KV_W), lambda b: (b, 0))],
        out_shape=[jax.ShapeDtypeStruct((N_CTX, ATT_W), F32),
                   jax.ShapeDtypeStruct((N_CTX, KV_W), F32),
                   jax.ShapeDtypeStruct((N_CTX, KV_W), F32)],
        compiler_params=_params("arbitrary"),
        name="attn_ctx",
    )(sink, p, p, p)


def _rope(x, cos, sin):
    lane = lax.broadcasted_iota(jnp.int32, x.shape, 1)
    swapped = jnp.where(lane % 32 < 16, pltpu.roll(x, 128 - 16, 1), pltpu.roll(x, 16, 1))
    return x * cos + swapped * sin


def _attn_lat_kernel(sink_ref, q_ref, k_ref, v_ref, ck_ref, cv_ref, cq_ref, sq_ref, cos_all_ref, sin_all_ref,
                     o_ref, kp_s, vp_s):
    i = pl.program_id(1)

    @pl.when(i == 0)
    def _():
        zeros = jnp.zeros((BLOCK, KV_W), BF16)
        kp_s[0:BLOCK, :] = zeros
        vp_s[0:BLOCK, :] = zeros
        kp_s[BLOCK + DEC_SEQ:, :] = zeros
        vp_s[BLOCK + DEC_SEQ:, :] = zeros
        kp_s[BLOCK:BLOCK + DEC_SEQ, :] = _rope(k_ref[...], cos_all_ref[...], sin_all_ref[...]).astype(BF16)
        vp_s[BLOCK:BLOCK + DEC_SEQ, :] = v_ref[...].astype(BF16)

    start = pl.multiple_of(i * BLOCK, BLOCK)
    kw = kp_s[pl.ds(start, 3 * BLOCK), :]
    vw = vp_s[pl.ds(start, 3 * BLOCK), :]
    ck = ck_ref[...].astype(BF16)
    cv = cv_ref[...].astype(BF16)
    qi = lax.broadcasted_iota(jnp.int32, (BLOCK, 3 * BLOCK), 0)
    ki = lax.broadcasted_iota(jnp.int32, (BLOCK, 3 * BLOCK), 1)
    rel = ki - BLOCK - qi
    kpos = ki + (start - BLOCK)
    ok = (jnp.abs(rel) <= WINDOW) & (kpos >= 0) & (kpos < DEC_SEQ)
    cq = cq_ref[...]
    sq = sq_ref[...]
    for pair in range(N_HEADS // 2):
        q2 = _rope(q_ref[:, pair * 128:(pair + 1) * 128], cq, sq).astype(BF16)
        for sub in range(2):
            hd = pair * 2 + sub
            kv = hd // GQA_G
            sk = sink_ref[hd]
            qh = q2[:, sub * HEAD_DIM:(sub + 1) * HEAD_DIM]
            sw = _dot_nt(qh, kw[:, kv * HEAD_DIM:(kv + 1) * HEAD_DIM]) * SCALE
            sw = jnp.where(ok, sw, NEG)
            sc = _dot_nt(qh, ck[:, kv * HEAD_DIM:(kv + 1) * HEAD_DIM]) * SCALE
            m = jnp.maximum(jnp.maximum(jnp.max(sw, axis=-1, keepdims=True),
                                        jnp.max(sc, axis=-1, keepdims=True)), sk)
            ew = jnp.exp(sw - m)
            ec = jnp.exp(sc - m)
            den = (jnp.sum(ew, axis=-1, keepdims=True) + jnp.sum(ec, axis=-1, keepdims=True)
                   + jnp.exp(sk - m))
            o = (_dot(ew.astype(BF16), vw[:, kv * HEAD_DIM:(kv + 1) * HEAD_DIM])
                 + _dot(ec.astype(BF16), cv[:, kv * HEAD_DIM:(kv + 1) * HEAD_DIM]))
            o_ref[:, hd * HEAD_DIM:(hd + 1) * HEAD_DIM] = o / den


def _attn_lat(p, sink, cache_k, cache_v):
    cos, sin = _rope_tables()
    nqb = DEC_SEQ // BLOCK
    q0 = N_CTX // BLOCK
    b0 = N_CTX // DEC_SEQ
    kcol = ATT_W // KV_W
    return pl.pallas_call(
        _attn_lat_kernel,
        grid=(DEC_BATCH, nqb),
        in_specs=[
            pl.BlockSpec(memory_space=pltpu.SMEM),
            pl.BlockSpec((BLOCK, ATT_W), lambda b, i: (q0 + b * nqb + i, 0)),
            pl.BlockSpec((DEC_SEQ, KV_W), lambda b, i: (b0 + b, kcol)),
            pl.BlockSpec((DEC_SEQ, KV_W), lambda b, i: (b0 + b, kcol + 1)),
            pl.BlockSpec((None, PAST_LEN, KV_W), lambda b, i: (b, 0, 0)),
            pl.BlockSpec((None, PAST_LEN, KV_W), lambda b, i: (b, 0, 0)),
            pl.BlockSpec((BLOCK, 128), lambda b, i: (i, 0)),
            pl.BlockSpec((BLOCK, 128), lambda b, i: (i, 0)),
            pl.BlockSpec((DEC_SEQ, 128), lambda b, i: (0, 0)),
            pl.BlockSpec((DEC_SEQ, 128), lambda b, i: (0, 0)),
        ],
        out_specs=pl.BlockSpec((BLOCK, ATT_W), lambda b, i: (b * nqb + i, 0)),
        out_shape=jax.ShapeDtypeStruct((N_LAT, ATT_W), F32),
        scratch_shapes=[pltpu.VMEM((DEC_SEQ + 2 * BLOCK, KV_W), BF16),
                        pltpu.VMEM((DEC_SEQ + 2 * BLOCK, KV_W), BF16)],
        compiler_params=_params("arbitrary", "arbitrary"),
        name="attn_lat",
    )(sink, p, p, p, cache_k, cache_v, cos, sin, cos, sin)


def _shift_rows(x, row, d):
    if d == 0:
        return x
    n = x.shape[0]
    y = pltpu.roll(x, d % n, 0)
    return jnp.where((row >= d) & (row < n + d), y, 0.0)


def _sigmoid(x):
    return 0.5 * jnp.tanh(0.5 * x) + 0.5


def _sqrt_nonneg(x):
    return jnp.where(x > 0.0, x * lax.rsqrt(x), 0.0)


def _gelu_tanh(x):
    return x * (0.5 * (1.0 + jnp.tanh(math.sqrt(2.0 / math.pi) * (x + 0.044715 * (x * x * x)))))


def _lru_kernel(xr_ref, g_ref, h0_ref, cw_ref, cb_ref, wg_ref, bg_ref, lam_ref, o_ref, st_ref,
                af_s, uf_s, ab_s, ub_s, hf_s, hb_s, *, L, G):
    sp = jnp.maximum(-lam_ref[...], 0.0) + jnp.log1p(jnp.exp(-jnp.abs(lam_ref[...])))
    nsp = -LRU_C * sp
    row = lax.broadcasted_iota(jnp.int32, (L, LCB), 0)
    left = LRU_CONV // 2

    def prep(b, carry):
        r0 = pl.multiple_of(b * L, L)
        x = xr_ref[pl.ds(r0, L), :]
        xc = cb_ref[...]
        for k in range(LRU_CONV):
            xc = xc + _shift_rows(x, row, left - k) * cw_ref[k:k + 1, :]
        gates = _dot(xc.astype(BF16), wg_ref[...]) + bg_ref[...]
        for d, (a_s, u_s) in enumerate(((af_s, uf_s), (ab_s, ub_s))):
            r = _sigmoid(gates[:, (2 * d) * LCB:(2 * d + 1) * LCB])
            gi = _sigmoid(gates[:, (2 * d + 1) * LCB:(2 * d + 2) * LCB])
            log_a = r * nsp[d:d + 1, :]
            a = jnp.exp(log_a)
            a_s[pl.ds(r0, L), :] = a
            u_s[pl.ds(r0, L), :] = _sqrt_nonneg(jnp.tanh(-log_a) * (1.0 + a * a)) * (gi * xc)
        return carry

    lax.fori_loop(0, G, prep, 0)

    def step(t, carry):
        hf, hb = carry
        tb = L - 1 - t
        hf = af_s[pl.ds(t, G, stride=L), :] * hf + uf_s[pl.ds(t, G, stride=L), :]
        hf_s[pl.ds(t, G, stride=L), :] = hf
        hb = ab_s[pl.ds(tb, G, stride=L), :] * hb + ub_s[pl.ds(tb, G, stride=L), :]
        hb_s[pl.ds(tb, G, stride=L), :] = hb
        return hf, hb

    hf, hb = lax.fori_loop(0, L, step, (h0_ref[0], h0_ref[1]), unroll=8)
    st_ref[0] = hf
    st_ref[1] = hb

    def fin(b, carry):
        r0 = pl.multiple_of(b * L, L)
        o_ref[pl.ds(r0, L), :] = (hf_s[pl.ds(r0, L), :] + hb_s[pl.ds(r0, L), :]) * _gelu_tanh(g_ref[pl.ds(r0, L), :])
        return carry

    lax.fori_loop(0, G, fin, 0)


def _lru(p, h0, conv_w, conv_b, wg, bg, lam, *, row0, L, B, G):
    nc = LRU_W // LCB
    xcol = (ATT_W + 2 * KV_W) // LCB
    gcol = xcol + nc
    rb0 = row0 // (G * L)
    kern = functools.partial(_lru_kernel, L=L, G=G)
    return pl.pallas_call(
        kern,
        grid=(B // G, nc),
        in_specs=[
            pl.BlockSpec((G * L, LCB), lambda i, c: (rb0 + i, xcol + c)),
            pl.BlockSpec((G * L, LCB), lambda i, c: (rb0 + i, gcol + c)),
            pl.BlockSpec((2, G, LCB), lambda i, c: (0, i, c)),
            pl.BlockSpec((LRU_CONV, LCB), lambda i, c: (0, c)),
            pl.BlockSpec((1, LCB), lambda i, c: (0, c)),
            pl.BlockSpec((None, LCB, 4 * LCB), lambda i, c: (c, 0, 0)),
            pl.BlockSpec((None, 1, 4 * LCB), lambda i, c: (c, 0, 0)),
            pl.BlockSpec((2, LCB), lambda i, c: (0, c)),
        ],
        out_specs=[
            pl.BlockSpec((G * L, LCB), lambda i, c: (i, c)),
            pl.BlockSpec((2, G, LCB), lambda i, c: (0, i, c)),
        ],
        out_shape=[jax.ShapeDtypeStruct((B * L, LRU_W), F32),
                   jax.ShapeDtypeStruct((2, B, LRU_W), F32)],
        scratch_shapes=[pltpu.VMEM((G * L, LCB), F32) for _ in range(6)],
        compiler_params=_params("arbitrary", "arbitrary"),
        name="lru",
    )(p, p, h0, conv_w, conv_b.reshape(1, LRU_W), wg, bg, lam)


LRU_TC = 32


def _lru_tm_kernel(xr_ref, g_ref, h0_ref, cw_ref, cb_ref, wg_ref, bg_ref, lam_ref, o_ref, st_ref,
                   xp_s, af_s, uf_s, ab_s, ub_s, hf_s, hb_s, *, L, G):
    sp = jnp.maximum(-lam_ref[...], 0.0) + jnp.log1p(jnp.exp(-jnp.abs(lam_ref[...])))
    nsp = -LRU_C * sp
    left = LRU_CONV // 2
    right = LRU_CONV - 1 - left
    xp_s[0:left] = jnp.zeros((left, G, LCB), F32)
    xp_s[left + L:left + L + right] = jnp.zeros((right, G, LCB), F32)
    xp_s[left:left + L] = jnp.swapaxes(xr_ref[...].reshape(G, L, LCB), 0, 1)

    def prep(c, carry):
        t0 = pl.multiple_of(c * LRU_TC, LRU_TC)
        xc = cb_ref[...].reshape(1, 1, LCB)
        for k in range(LRU_CONV):
            xc = xc + xp_s[pl.ds(t0 + k, LRU_TC)] * cw_ref[k:k + 1, :].reshape(1, 1, LCB)
        xc = xc.reshape(LRU_TC * G, LCB)
        gates = _dot(xc.astype(BF16), wg_ref[...]) + bg_ref[...]
        for d, (a_s, u_s) in enumerate(((af_s, uf_s), (ab_s, ub_s))):
            r = _sigmoid(gates[:, (2 * d) * LCB:(2 * d + 1) * LCB])
            gi = _sigmoid(gates[:, (2 * d + 1) * LCB:(2 * d + 2) * LCB])
            log_a = r * nsp[d:d + 1, :]
            a = jnp.exp(log_a)
            a_s[pl.ds(t0, LRU_TC)] = a.reshape(LRU_TC, G, LCB)
            u = _sqrt_nonneg(jnp.tanh(-log_a) * (1.0 + a * a)) * (gi * xc)
            u_s[pl.ds(t0, LRU_TC)] = u.reshape(LRU_TC, G, LCB)
        return carry

    lax.fori_loop(0, L // LRU_TC, prep, 0)

    def step(t, carry):
        hf, hb = carry
        tb = L - 1 - t
        hf = af_s[t] * hf + uf_s[t]
        hf_s[t] = hf
        hb = ab_s[tb] * hb + ub_s[tb]
        hb_s[tb] = hb
        return hf, hb

    hf, hb = lax.fori_loop(0, L, step, (h0_ref[0], h0_ref[1]), unroll=8)
    st_ref[0] = hf
    st_ref[1] = hb
    h = jnp.swapaxes(hf_s[...] + hb_s[...], 0, 1).reshape(G * L, LCB)
    o_ref[...] = h * _gelu_tanh(g_ref[...])


def _lru_tm(p, h0, conv_w, conv_b, wg, bg, lam, *, row0, L, B, G):
    nc = LRU_W // LCB
    xcol = (ATT_W + 2 * KV_W) // LCB
    gcol = xcol + nc
    rb0 = row0 // (G * L)
    tm = lambda extra: pltpu.VMEM((L + extra, G, LCB), F32)
    return pl.pallas_call(
        functools.partial(_lru_tm_kernel, L=L, G=G),
        grid=(B // G, nc),
        in_specs=[
            pl.BlockSpec((G * L, LCB), lambda i, c: (rb0 + i, xcol + c)),
            pl.BlockSpec((G * L, LCB), lambda i, c: (rb0 + i, gcol + c)),
            pl.BlockSpec((2, G, LCB), lambda i, c: (0, i, c)),
            pl.BlockSpec((LRU_CONV, LCB), lambda i, c: (0, c)),
            pl.BlockSpec((1, LCB), lambda i, c: (0, c)),
            pl.BlockSpec((None, LCB, 4 * LCB), lambda i, c: (c, 0, 0)),
            pl.BlockSpec((None, 1, 4 * LCB), lambda i, c: (c, 0, 0)),
            pl.BlockSpec((2, LCB), lambda i, c: (0, c)),
        ],
        out_specs=[
            pl.BlockSpec((G * L, LCB), lambda i, c: (i, c)),
            pl.BlockSpec((2, G, LCB), lambda i, c: (0, i, c)),
        ],
        out_shape=[jax.ShapeDtypeStruct((B * L, LRU_W), F32),
                   jax.ShapeDtypeStruct((2, B, LRU_W), F32)],
        scratch_shapes=[tm(LRU_CONV - 1)] + [tm(0) for _ in range(6)],
        compiler_params=_params("arbitrary", "arbitrary"),
        name="lru_tm",
    )(p, p, h0, conv_w, conv_b.reshape(1, LRU_W), wg, bg, lam)


def _lru_gate_weights(w_r, b_r, w_i, b_i):
    nc = LRU_W // LCB
    per = LCB // LRU_BD
    eye = jnp.eye(per, dtype=F32)

    def dense(w):
        w = w.reshape(nc, per, LRU_BD, LRU_BD)
        d = jnp.einsum('chij,hg->chigj', w, eye)
        return d.reshape(nc, LCB, LCB)

    cols = [dense(w_r[0]), dense(w_i[0]), dense(w_r[1]), dense(w_i[1])]
    wg = jnp.concatenate(cols, axis=-1).astype(BF16)
    bias = [b.reshape(nc, 1, LCB) for b in (b_r[0], b_i[0], b_r[1], b_i[1])]
    return wg, jnp.concatenate(bias, axis=-1)


def _fnet_kernel(f_ref, seq_ref, grp_ref, o_ref, cs_s, *, L, G):
    grp = grp_ref[...]
    for b in range(G):
        for gi in range(FNET_GROUPS):
            f = f_ref[b * L:(b + 1) * L, gi * FNET_GD:(gi + 1) * FNET_GD].astype(BF16)
            cs = _dot(f, grp).astype(BF16)
            col = b * FNET_W + gi * FNET_GD
            cs_s[0:L, col:col + FNET_GD] = cs[:, :FNET_GD]
            cs_s[L:2 * L, col:col + FNET_GD] = cs[:, FNET_GD:]
    y = _dot(seq_ref[...], cs_s[...])
    for b in range(G):
        o_ref[b * L:(b + 1) * L, :] = y[:, b * FNET_W:(b + 1) * FNET_W]


def _fnet(p, *, row0, L, B, G):
    seq, grp = _fnet_tables(L)
    rb0 = row0 // (G * L)
    return pl.pallas_call(
        functools.partial(_fnet_kernel, L=L, G=G),
        grid=(B // G,),
        in_specs=[
            pl.BlockSpec((G * L, FNET_W), lambda b: (rb0 + b, 0)),
            _resident((L, 2 * L), lambda b: (0, 0)),
            _resident((FNET_GD, 2 * FNET_GD), lambda b: (0, 0)),
        ],
        out_specs=pl.BlockSpec((G * L, FNET_W), lambda b: (b, 0)),
        out_shape=jax.ShapeDtypeStruct((B * L, FNET_W), F32),
        scratch_shapes=[pltpu.VMEM((2 * L, G * FNET_W), BF16)],
        compiler_params=_params("arbitrary"),
        name="fnet",
    )(p, _table_bf16(seq), _table_bf16(grp))


def _alternating_sign(row):
    return (1 - 2 * (row & 1)).astype(F32)


def _filter_kernel(zh_ref, zl_ref, w1_ref, b1_ref, w2_ref, b2_ref, w3f_ref, w3b_ref, fr_ref, ldf_ref, ldb_ref,
                   fc_ref, fs_ref, fcb_ref, fsb_ref, k1_ref, k2_ref, kn_ref, hid_hi_s, hid_lo_s, *, L):
    @pl.when((pl.program_id(0) == 0) & (pl.program_id(1) == 0))
    def _():
        hid = jnp.sin(fr_ref[0:1, :] * (_dot3((zh_ref[...], zl_ref[...]), _split(w1_ref[...])) + b1_ref[...]))
        hid = jnp.sin(fr_ref[1:2, :] * (_dot3(_split(hid), _split(w2_ref[...])) + b2_ref[...]))
        hid_hi_s[...], hid_lo_s[...] = _split(hid)

    hid = (hid_hi_s[...], hid_lo_s[...])
    tn = lax.broadcasted_iota(jnp.int32, (L, CB), 0).astype(F32) * (1.0 / L)
    ff = _dot3(hid, _split(w3f_ref[...])) * jnp.exp(-tn * jnp.exp(ldf_ref[...]))
    fb = _dot3(hid, _split(w3b_ref[...])) * jnp.exp(-tn * jnp.exp(ldb_ref[...]))
    ss = jnp.sum(ff * ff, axis=0, keepdims=True) + jnp.sum(fb * fb, axis=0, keepdims=True)
    scale = lax.rsqrt(ss + EPS)
    row = lax.broadcasted_iota(jnp.int32, (L, CB), 0)
    sgn = _alternating_sign(row)
    nyq = jnp.sum(sgn * ff, axis=0, keepdims=True) - jnp.sum(sgn * fb, axis=0, keepdims=True)
    ff = ff.astype(BF16)
    fb = fb.astype(BF16)
    kr = _dot(fc_ref[...], ff) + _dot(fcb_ref[...], fb)
    ki = _dot(fs_ref[...], ff) + _dot(fsb_ref[...], fb)
    w = jnp.where(row == 0, 0.5 / L, 1.0 / L) * scale
    k1_ref[0] = kr * w
    k2_ref[0] = ki * w
    kn_ref[0] = nyq * (scale * (0.5 / L))


def _hyena_filters(L, w1, b1, w2, b2, w3, freq, log_decay):
    zh, zl = _hyena_embedding(L)
    tabs = [_table_bf16(t) for t in _rdft_tables(L)]
    nc = HY_W // CB
    w1p = jnp.zeros((EMB_PAD, HY_HID), F32).at[:HY_EMB].set(w1)
    bcol = HY_ORDER * nc
    tab_spec = _resident((L, L), lambda n, c: (0, 0))
    small = lambda shape: _resident(shape, lambda n, c: (0, 0))
    return pl.pallas_call(
        functools.partial(_filter_kernel, L=L),
        grid=(HY_ORDER, nc),
        in_specs=[
            small((L, EMB_PAD)), small((L, EMB_PAD)),
            small((EMB_PAD, HY_HID)), small((1, HY_HID)),
            small((HY_HID, HY_HID)), small((1, HY_HID)),
            pl.BlockSpec((HY_HID, CB), lambda n, c: (0, n * nc + c)),
            pl.BlockSpec((HY_HID, CB), lambda n, c: (0, bcol + n * nc + c)),
            small((2, HY_HID)),
            pl.BlockSpec((1, CB), lambda n, c: (0, n * nc + c)),
            pl.BlockSpec((1, CB), lambda n, c: (0, bcol + n * nc + c)),
        ] + [tab_spec] * 4,
        out_specs=[
            pl.BlockSpec((1, L, CB), lambda n, c: (n, 0, c)),
            pl.BlockSpec((1, L, CB), lambda n, c: (n, 0, c)),
            pl.BlockSpec((1, 1, CB), lambda n, c: (n, 0, c)),
        ],
        out_shape=[jax.ShapeDtypeStruct((HY_ORDER, L, HY_W), F32),
                   jax.ShapeDtypeStruct((HY_ORDER, L, HY_W), F32),
                   jax.ShapeDtypeStruct((HY_ORDER, 1, HY_W), F32)],
        scratch_shapes=[pltpu.VMEM((L, HY_HID), BF16), pltpu.VMEM((L, HY_HID), BF16)],
        compiler_params=_params("arbitrary", "arbitrary"),
        name="hyena_filter",
    )(zh, zl, w1p, b1.reshape(1, HY_HID), w2, b2.reshape(1, HY_HID), w3, w3, freq,
      log_decay.reshape(1, HY_FILT), log_decay.reshape(1, HY_FILT), *tabs)


def _hyena_kernel(v_ref, x1_ref, x2_ref, cwv_ref, cw1_ref, cw2_ref, cbv_ref, cb1_ref, cb2_ref, bias_ref,
                  k1_ref, k2_ref, kn_ref, fc_ref, fs_ref, o_ref, *, L, G):
    row = lax.broadcasted_iota(jnp.int32, (L, CB), 0)
    left = HY_CONV // 2

    def conv(x_ref, cw_ref, cb_ref):
        outs = []
        for g in range(G):
            x = x_ref[g * L:(g + 1) * L, :]
            y = cb_ref[...]
            for k in range(HY_CONV):
                y = y + _shift_rows(x, row, left - k) * cw_ref[k:k + 1, :]
            outs.append(y)
        return outs[0] if G == 1 else jnp.concatenate(outs, axis=1)

    def per_seq(x):
        return x if G == 1 else jnp.concatenate([x] * G, axis=1)

    fc = fc_ref[...]
    fs = fs_ref[...]
    sgn = per_seq(_alternating_sign(row))
    z = conv(v_ref, cwv_ref, cbv_ref)
    gates = (conv(x1_ref, cw1_ref, cb1_ref), conv(x2_ref, cw2_ref, cb2_ref))
    for n in range(HY_ORDER):
        zb = z.astype(BF16)
        a = _dot(fc, zb)
        b = _dot(fs, zb)
        nyq = jnp.sum(sgn * z, axis=0, keepdims=True)
        k1 = per_seq(k1_ref[n])
        k2 = per_seq(k2_ref[n])
        p = a * k1 - b * k2
        q = a * k2 + b * k1
        y = _dot(fc, p.astype(BF16)) + _dot(fs, q.astype(BF16)) + sgn * (nyq * per_seq(kn_ref[n]))
        z = gates[n] * (y + per_seq(bias_ref[n:n + 1, :]) * z)
    for g in range(G):
        o_ref[g * L:(g + 1) * L, :] = z[:, g * CB:(g + 1) * CB]


def _hyena(p, conv_w, conv_b, hy_bias, k1, k2, kn, *, row0, L, B, G):
    fc, fs = (_table_bf16(t) for t in _rdft_tables(L)[:2])
    nc = HY_W // CB
    rb0 = row0 // (G * L)
    c0 = FNET_W // CB
    conv_b = conv_b.reshape(1, (HY_ORDER + 1) * HY_W)
    x_spec = lambda j: pl.BlockSpec((G * L, CB), lambda c, b: (rb0 + b, c0 + j * nc + c))
    cw_spec = lambda j: pl.BlockSpec((HY_CONV, CB), lambda c, b: (0, j * nc + c))
    cb_spec = lambda j: pl.BlockSpec((1, CB), lambda c, b: (0, j * nc + c))
    k_spec = pl.BlockSpec((HY_ORDER, L, CB), lambda c, b: (0, 0, c))
    return pl.pallas_call(
        functools.partial(_hyena_kernel, L=L, G=G),
        grid=(nc, B // G),
        in_specs=[x_spec(0), x_spec(1), x_spec(2), cw_spec(0), cw_spec(1), cw_spec(2),
                  cb_spec(0), cb_spec(1), cb_spec(2),
                  pl.BlockSpec((HY_ORDER, CB), lambda c, b: (0, c)),
                  k_spec, k_spec, pl.BlockSpec((HY_ORDER, 1, CB), lambda c, b: (0, 0, c))]
                 + [_resident((L, L), lambda c, b: (0, 0))] * 2,
        out_specs=pl.BlockSpec((G * L, CB), lambda c, b: (b, c)),
        out_shape=jax.ShapeDtypeStruct((B * L, HY_W), F32),
        compiler_params=_params("arbitrary", "arbitrary"),
        name="hyena",
    )(p, p, p, conv_w, conv_w, conv_w, conv_b, conv_b, conv_b, hy_bias, k1, k2, kn, fc, fs)


def kernel(x_prompt, x_sample, c, cache_k, cache_v, state_lru, c_ctx, mod_w, mod_b, norm_mix, norm_mlp, norm_final, mlp_w1, mlp_w2, ev_w_in, ev_w_out, attn_sink, lru_conv_w, lru_conv_b, lru_w_r, lru_b_r, lru_w_i, lru_b_i, lru_lambda, od_w_in, od_w_out, hy_conv_w, hy_conv_b, hy_w1, hy_b1, hy_w2, hy_b2, hy_w3, hy_freq, hy_log_decay, hy_bias):
    x = (x_prompt.reshape(N_CTX, D_MODEL), x_sample.reshape(N_LAT, D_MODEL))
    cvec = jnp.concatenate([c_ctx[None, :], c, jnp.zeros((8 - N_GROUPS, D_MODEL), F32)], axis=0)
    mods = _modulation(cvec, mod_w, mod_b)
    zero_state = jnp.zeros((2, BATCH, LRU_W), F32)
    ev_w_in_b, ev_w_out_b = ev_w_in.astype(BF16), ev_w_out.astype(BF16)
    od_w_in_b, od_w_out_b = od_w_in.astype(BF16), od_w_out.astype(BF16)
    w1_b, w2_b = mlp_w1[0].astype(BF16), mlp_w2[0].astype(BF16)
    k_list, v_list, s_list = [], [], []
    h_next = None
    for l in range(DEPTH):
        j = l // 2
        w_in = ev_w_in_b if l % 2 == 0 else od_w_in_b
        p = _pre(x, norm_mix[l], mods[l], w_in, j) if l == 0 else _project(h_next, w_in, j)
        if l % 2 == 0:
            att_ctx, k_ctx, v_ctx = _attn_ctx(p, attn_sink[j])
            k_list.append(k_ctx.reshape(BATCH, SEQ, N_KV, HEAD_DIM))
            v_list.append(v_ctx.reshape(BATCH, SEQ, N_KV, HEAD_DIM))
            att_lat = _attn_lat(p, attn_sink[j], cache_k[:, j].reshape(DEC_BATCH, PAST_LEN, KV_W),
                                cache_v[:, j].reshape(DEC_BATCH, PAST_LEN, KV_W))
            wg, bg = _lru_gate_weights(lru_w_r[j], lru_b_r[j], lru_w_i[j], lru_b_i[j])
            lru_args = (lru_conv_w[j], lru_conv_b[j], wg, bg, lru_lambda[j])
            y_ctx, st = _lru_tm(p, zero_state, *lru_args, row0=0, L=SEQ, B=BATCH, G=8)
            h0_lat = jnp.swapaxes(state_lru[:, j], 0, 1)
            y_lat, _ = _lru(p, h0_lat, *lru_args, row0=N_CTX, L=DEC_SEQ, B=DEC_BATCH, G=DEC_BATCH)
            s_list.append(jnp.swapaxes(st, 0, 1))
            m1, m2 = (att_ctx, att_lat), (y_ctx, y_lat)
            w_out = ev_w_out_b
        else:
            filt =(hy_w1[j], hy_b1[j], hy_w2[j], hy_b2[j], hy_w3[j], hy_freq[j], hy_log_decay[j])
            m1 = (_fnet(p, row0=0, L=SEQ, B=BATCH, G=4), _fnet(p, row0=N_CTX, L=DEC_SEQ, B=DEC_BATCH, G=1))
            hy = []
            for row0, L, B, G in ((0, SEQ, BATCH, 4), (N_CTX, DEC_SEQ, DEC_BATCH, 2)):
                k1, k2, kn = _hyena_filters(L, *filt)
                hy.append(_hyena(p, hy_conv_w[j], hy_conv_b[j], hy_bias[j], k1, k2, kn, row0=row0, L=L, B=B, G=G))
            m2 = tuple(hy)
            w_out = od_w_out_b
        last = l == DEPTH - 1
        post = functools.partial(_post, x, m1, m2, mods[l], mods[l if last else l + 1], norm_mlp[l],
                                 norm_final if last else norm_mix[l + 1], w_out, w1_b, w2_b, mlp_w1, mlp_w2, j, l)
        if not last:
            x, h_next, w1_b, w2_b = post(final=False)
        else:
            y_prompt = post(final=True, tile0=0, n_tiles=N_CTX_TILES).reshape(BATCH, SEQ, D_MODEL)
            y_sample = post(final=True, tile0=N_CTX_TILES, n_tiles=N_LAT // TM).reshape(DEC_BATCH, DEC_SEQ, D_MODEL)
    k_state = jnp.stack(k_list, axis=1)
    v_state = jnp.stack(v_list, axis=1)
    lru_state = jnp.stack(s_list, axis=1).astype(x_prompt.dtype)
    return (y_prompt, y_sample, k_state, v_state, lru_state)
```

```python
import functools
import math

import numpy as np
import jax
import jax.numpy as jnp
from jax import lax
from jax.experimental import pallas as pl
from jax.experimental.pallas import tpu as pltpu

F32 = jnp.float32
BF16 = jnp.bfloat16

D_MODEL = 1024
BATCH = 32
SEQ = 256
DEPTH = 4
DEC_BATCH = 2
DEC_SEQ = 1024
PAST_LEN = 256
GRID_W = 64
N_HEADS = 8
N_KV = 2
HEAD_DIM = 64
GQA_G = N_HEADS // N_KV
ATT_W = N_HEADS * HEAD_DIM
KV_W = N_KV * HEAD_DIM
WINDOW = 128
BLOCK = 128
ROPE_BASE = 10000.0
LRU_W = 512
LRU_BLOCKS = 8
LRU_BD = LRU_W // LRU_BLOCKS
LRU_CONV = 4
LRU_C = 8.0
FNET_GROUPS = 4
FNET_W = 512
FNET_GD = FNET_W // FNET_GROUPS
HY_W = 512
HY_ORDER = 2
HY_CONV = 3
HY_BANDS = 16
HY_EMB = 1 + 2 * HY_BANDS
HY_HID = 64
HY_FILT = 2 * HY_ORDER * HY_W
EVEN_IN = ATT_W + 2 * KV_W + 2 * LRU_W
ODD_IN = FNET_W + (HY_ORDER + 1) * HY_W
D_FF = 4 * D_MODEL
N_MOD = 6
EPS = 1e-6
NEG = -1e30

N_CTX = BATCH * SEQ
N_LAT = DEC_BATCH * DEC_SEQ
N_TOK = N_CTX + N_LAT
N_GROUPS = 1 + DEC_BATCH

VMEM_LIMIT = 56 * 1024 * 1024
TM = 512
CB = 256
LCB = 128
FF_CHUNK = 1024
EMB_PAD = 128


def _params(*sem):
    return pltpu.CompilerParams(dimension_semantics=sem, vmem_limit_bytes=VMEM_LIMIT)


def _resident(shape, index_map):
    return pl.BlockSpec(shape, index_map, pipeline_mode=pl.Buffered(1))


def _dot(a, b):
    return jnp.dot(a, b, preferred_element_type=F32)


def _dot_nt(a, b):
    return lax.dot_general(a, b, (((1,), (1,)), ((), ())), preferred_element_type=F32)


def _split(x):
    hi = x.astype(BF16)
    lo = (x - hi.astype(F32)).astype(BF16)
    return hi, lo


def _dot3(a, b):
    return _dot(a[0], b[0]) + (_dot(a[1], b[0]) + _dot(a[0], b[1]))


def _np_f32(x):
    return np.asarray(x, dtype=np.float32)


def _table_bf16(t):
    return jnp.asarray(t).astype(BF16)


def _np_split(x):
    hi = np.asarray(x, dtype=np.float32).astype(BF16)
    lo = (np.asarray(x, dtype=np.float64) - hi.astype(np.float64)).astype(np.float32).astype(BF16)
    return hi, lo


def _group_of_tile(i):
    return jnp.maximum(i * TM - (N_CTX - DEC_SEQ), 0) // DEC_SEQ


@functools.lru_cache(maxsize=None)
def _rdft_tables(L):
    k = np.arange(L, dtype=np.int64)[:, None]
    t = np.arange(L, dtype=np.int64)[None, :]
    ang = (k * t % (2 * L)).astype(np.float64) * (math.pi / L)
    tb = 2 * L - 1 - t
    angb = (k * tb % (2 * L)).astype(np.float64) * (math.pi / L)
    return tuple(_np_f32(a) for a in (np.cos(ang), -np.sin(ang), np.cos(angb), -np.sin(angb)))


@functools.lru_cache(maxsize=None)
def _fnet_tables(L):
    k = np.arange(L, dtype=np.int64)[:, None]
    t = np.arange(L, dtype=np.int64)[None, :]
    ang = (k * t % L).astype(np.float64) * (2.0 * math.pi / L)
    scale = 1.0 / math.sqrt(L * FNET_GD)
    seq = np.concatenate([np.cos(ang), -np.sin(ang)], axis=1) * scale
    m = np.arange(FNET_GD, dtype=np.int64)[:, None]
    d = np.arange(FNET_GD, dtype=np.int64)[None, :]
    angg = (m * d % FNET_GD).astype(np.float64) * (2.0 * math.pi / FNET_GD)
    grp = np.concatenate([np.cos(angg), np.sin(angg)], axis=1)
    return _np_f32(seq), _np_f32(grp)


@functools.lru_cache(maxsize=None)
def _rope_tables():
    L = DEC_SEQ
    n = HEAD_DIM // 4
    row = np.repeat(np.arange(L // GRID_W), GRID_W).astype(np.float32)
    col = np.tile(np.arange(GRID_W), L // GRID_W).astype(np.float32)
    inv = (np.float32(ROPE_BASE) ** (-np.arange(n, dtype=np.float32) / np.float32(n))).astype(np.float32)
    ar = (row[:, None] * inv[None, :]).astype(np.float64)
    ac = (col[:, None] * inv[None, :]).astype(np.float64)
    cos = np.concatenate([np.cos(ar), np.cos(ar), np.cos(ac), np.cos(ac)], axis=1)
    sin = np.concatenate([-np.sin(ar), np.sin(ar), -np.sin(ac), np.sin(ac)], axis=1)
    cos = np.tile(cos, (1, 2)).astype(np.float32)
    sin = np.tile(sin, (1, 2)).astype(np.float32)
    return cos, sin


@functools.lru_cache(maxsize=None)
def _hyena_embedding(L):
    t = np.arange(L, dtype=np.float32)
    tn = (t / np.float32(L)).astype(np.float64)
    bands = np.linspace(1e-4, HY_BANDS - 1, HY_BANDS, dtype=np.float32).astype(np.float64)
    w = (np.float32(2.0 * math.pi / L) * t).astype(np.float64)
    z = np.concatenate([tn[:, None], np.cos(w[:, None] * bands), -np.sin(w[:, None] * bands)], axis=-1)
    zp = np.zeros((L, EMB_PAD), np.float64)
    zp[:, :HY_EMB] = z
    return _np_split(zp)


MOD_TN = 1536


def _mod_kernel(c_ref, w_ref, b_ref, o_ref):
    c = c_ref[...]
    s_hi, s_lo = _split(c * jax.nn.sigmoid(c))
    r = _dot(jnp.concatenate([s_hi, s_lo], axis=0), w_ref[0].astype(BF16))
    o_ref[0] = r[:8] + r[8:] + b_ref[0]


def _modulation(cvec, mod_w, mod_b):
    out = pl.pallas_call(
        _mod_kernel,
        grid=(DEPTH, N_MOD * D_MODEL // MOD_TN),
        in_specs=[
            pl.BlockSpec((8, D_MODEL), lambda l, n: (0, 0)),
            pl.BlockSpec((1, D_MODEL, MOD_TN), lambda l, n: (l, 0, n)),
            pl.BlockSpec((1, 1, MOD_TN), lambda l, n: (l, 0, n)),
        ],
        out_specs=pl.BlockSpec((1, 8, MOD_TN), lambda l, n: (l, 0, n)),
        out_shape=jax.ShapeDtypeStruct((DEPTH, 8, N_MOD * D_MODEL), F32),
        compiler_params=_params("arbitrary", "arbitrary"),
        name="modulation",
    )(cvec, mod_w, mod_b.reshape(DEPTH, 1, N_MOD * D_MODEL))
    return out.reshape(DEPTH, 8, N_MOD, D_MODEL)[:, :N_GROUPS]


def _rms(x, g):
    return x * lax.rsqrt(jnp.mean(x * x, axis=-1, keepdims=True) + EPS) * g


N_CTX_TILES = N_CTX // TM


def _row_stream(x, tile0):
    if not isinstance(x, tuple):
        return [x], [pl.BlockSpec((TM, x.shape[1]), lambda i: (i + tile0, 0))]
    ctx, lat = x
    return [ctx, lat], [
        pl.BlockSpec((TM, ctx.shape[1]), lambda i: (jnp.minimum(i + tile0, N_CTX_TILES - 1), 0)),
        pl.BlockSpec((TM, lat.shape[1]), lambda i: (jnp.maximum(i + tile0 - N_CTX_TILES, 0), 0)),
    ]


def _read_rows(refs, paired, tile):
    vals = []
    k = 0
    for p in paired:
        if p:
            vals.append(jnp.where(tile < N_CTX_TILES, refs[k][...], refs[k + 1][...]))
            k += 2
        else:
            vals.append(refs[k][...])
            k += 1
    return vals, refs[k:]


def _pre_kernel(*refs, paired, tile0):
    (x,), (g_ref, mod_ref, w_ref, o_ref) = _read_rows(refs, paired, pl.program_id(0) + tile0)
    h = _rms(x, g_ref[...]) * (1.0 + mod_ref[1:2, :]) + mod_ref[0:1, :]
    o_ref[...] = _dot(h.astype(BF16), w_ref[...])


def _pre(x, g, mod, w_in, j):
    n_in = w_in.shape[2]
    ops, specs = _row_stream(x, 0)
    return pl.pallas_call(
        functools.partial(_pre_kernel, paired=(isinstance(x, tuple),), tile0=0),
        grid=(N_TOK // TM,),
        in_specs=specs + [
            _resident((1, D_MODEL), lambda i: (0, 0)),
            pl.BlockSpec((None, N_MOD, D_MODEL), lambda i: (_group_of_tile(i), 0, 0)),
            _resident((None, D_MODEL, n_in), lambda i: (j, 0, 0)),
        ],
        out_specs=pl.BlockSpec((TM, n_in), lambda i: (i, 0)),
        out_shape=jax.ShapeDtypeStruct((N_TOK, n_in), F32),
        compiler_params=_params("arbitrary"),
        name="pre",
    )(*ops, g.reshape(1, D_MODEL), mod, w_in)


def _post_kernel(*refs, paired, tile0, final):
    (x, m1, m2), rest = _read_rows(refs, paired, pl.program_id(0) + tile0)
    mod_ref, modn_ref, g_ref, gf_ref, wo_ref, w1_ref, w2_ref = rest[:7]
    if final:
        (o_ref,) = rest[7:]
    else:
        w1f_ref, w2f_ref, o_ref, hn_ref, w1n_ref, w2n_ref = rest[7:]
    half = wo_ref.shape[0] // 2
    o = _dot(m1.astype(BF16), wo_ref[:half, :]) + _dot(m2.astype(BF16), wo_ref[half:, :])
    x1 = x + mod_ref[2:3, :] * o
    h = (_rms(x1, g_ref[...]) * (1.0 + mod_ref[4:5, :]) + mod_ref[3:4, :]).astype(BF16)
    acc = jnp.zeros((TM, D_MODEL), F32)
    for c in range(D_FF // FF_CHUNK):
        t = jnp.maximum(_dot(h, w1_ref[:, c * FF_CHUNK:(c + 1) * FF_CHUNK]), 0.0)
        acc = acc + _dot((t * t).astype(BF16), w2_ref[c * FF_CHUNK:(c + 1) * FF_CHUNK, :])
    x2 = x1 + mod_ref[5:6, :] * acc
    if final:
        o_ref[...] = _rms(x2, gf_ref[...])
    else:
        o_ref[...] = x2
        hn_ref[...] = (_rms(x2, gf_ref[...]) * (1.0 + modn_ref[1:2, :]) + modn_ref[0:1, :]).astype(BF16)
        w1n_ref[...] = w1f_ref[...].astype(BF16)
        w2n_ref[...] = w2f_ref[...].astype(BF16)


W_CHUNKS = 16


def _post(x, m1, m2, mod, mod_next, g, g_tail, w_out, w1, w2, w1_f32, w2_f32, j, l, *, final, tile0=0,
          n_tiles=N_TOK // TM):
    mod_spec = pl.BlockSpec((None, N_MOD, D_MODEL), lambda i: (_group_of_tile(i + tile0), 0, 0))
    row_spec = pl.BlockSpec((TM, D_MODEL), lambda i: (i, 0))
    x_shape = jax.ShapeDtypeStruct((n_tiles * TM, D_MODEL), F32)
    extra_in, extra_ops, extra_out, extra_shapes = [], [], [], []
    if not final:
        assert n_tiles >= W_CHUNKS
        cw = D_FF // W_CHUNKS
        chunk = lambda i: jnp.minimum(i, W_CHUNKS - 1)
        extra_in = [pl.BlockSpec((None, D_MODEL, cw), lambda i: (l + 1, 0, chunk(i))),
                    pl.BlockSpec((None, cw, D_MODEL), lambda i: (l + 1, chunk(i), 0))]
        extra_ops = [w1_f32, w2_f32]
        extra_out = [row_spec, pl.BlockSpec((D_MODEL, cw), lambda i: (0, chunk(i))),
                     pl.BlockSpec((cw, D_MODEL), lambda i: (chunk(i), 0))]
        extra_shapes = [jax.ShapeDtypeStruct((n_tiles * TM, D_MODEL), BF16),
                        jax.ShapeDtypeStruct((D_MODEL, D_FF), BF16), jax.ShapeDtypeStruct((D_FF, D_MODEL), BF16)]
    ops, specs, paired = [], [], []
    for s in (x, m1, m2):
        o, sp = _row_stream(s, tile0)
        ops += o
        specs += sp
        paired.append(isinstance(s, tuple))
    return pl.pallas_call(
        functools.partial(_post_kernel, paired=tuple(paired), tile0=tile0, final=final),
        grid=(n_tiles,),
        in_specs=specs + [
            mod_spec,
            mod_spec,
            _resident((1, D_MODEL), lambda i: (0, 0)),
            _resident((1, D_MODEL), lambda i: (0, 0)),
            _resident((None,) + w_out.shape[1:], lambda i: (j, 0, 0)),
            _resident(w1.shape, lambda i: (0, 0)),
            _resident(w2.shape, lambda i: (0, 0)),
        ] + extra_in,
        out_specs=row_spec if final else [row_spec] + extra_out,
        out_shape=x_shape if final else [x_shape] + extra_shapes,
        compiler_params=_params("arbitrary"),
        name="post",
    )(*ops, mod, mod_next, g.reshape(1, D_MODEL), g_tail.reshape(1, D_MODEL), w_out, w1, w2, *extra_ops)


def _project_kernel(h_ref, w_ref, o_ref):
    o_ref[...] = _dot(h_ref[...], w_ref[...])


def _project(h, w_in, j):
    n_in = w_in.shape[2]
    return pl.pallas_call(
        _project_kernel,
        grid=(N_TOK // TM,),
        in_specs=[pl.BlockSpec((TM, D_MODEL), lambda i: (i, 0)),
                  _resident((None, D_MODEL, n_in), lambda i: (j, 0, 0))],
        out_specs=pl.BlockSpec((TM, n_in), lambda i: (i, 0)),
        out_shape=jax.ShapeDtypeStruct((N_TOK, n_in), F32),
        compiler_params=_params("arbitrary"),
        name="project",
    )(h, w_in)


SCALE = HEAD_DIM ** -0.5


def _attn_ctx_kernel(sink_ref, q_ref, k_ref, v_ref, o_ref, ko_ref, vo_ref):
    ko_ref[...] = k_ref[...]
    vo_ref[...] = v_ref[...]
    k = k_ref[...].astype(BF16)
    v = v_ref[...].astype(BF16)
    for hd in range(N_HEADS):
        kv = hd // GQA_G
        sk = sink_ref[hd]
        qh = q_ref[:, hd * HEAD_DIM:(hd + 1) * HEAD_DIM].astype(BF16)
        s = _dot_nt(qh, k[:, kv * HEAD_DIM:(kv + 1) * HEAD_DIM]) * SCALE
        m = jnp.maximum(jnp.max(s, axis=-1, keepdims=True), sk)
        e = jnp.exp(s - m)
        den = jnp.sum(e, axis=-1, keepdims=True) + jnp.exp(sk - m)
        o = _dot(e.astype(BF16), v[:, kv * HEAD_DIM:(kv + 1) * HEAD_DIM])
        o_ref[:, hd * HEAD_DIM:(hd + 1) * HEAD_DIM] = o / den


def _attn_ctx(p, sink):
    kcol = ATT_W // KV_W
    return pl.pallas_call(
        _attn_ctx_kernel,
        grid=(BATCH,),
        in_specs=[
            pl.BlockSpec(memory_space=pltpu.SMEM),
            pl.BlockSpec((SEQ, ATT_W), lambda b: (b, 0)),
            pl.BlockSpec((SEQ, KV_W), lambda b: (b, kcol)),
            pl.BlockSpec((SEQ, KV_W), lambda b: (b, kcol + 1)),
        ],
        out_specs=[pl.BlockSpec((SEQ, ATT_W), lambda b: (b, 0)),
                   pl.BlockSpec((SEQ, KV_W), lambda b: (b, 0)),
                   pl.BlockSpec((SEQ, KV_W), lambda b: (b, 0))],
        out_shape=[jax.ShapeDtypeStruct((N_CTX, ATT_W), F32),
                   jax.ShapeDtypeStruct((N_CTX, KV_W), F32),
                   jax.ShapeDtypeStruct((N_CTX, KV_W), F32)],
        compiler_params=_params("arbitrary"),
        name="attn_ctx",
    )(sink, p, p, p)


def _rope(x, cos, sin):
    lane = lax.broadcasted_iota(jnp.int32, x.shape, 1)
    swapped = jnp.where(lane % 32 < 16, pltpu.roll(x, 128 - 16, 1), pltpu.roll(x, 16, 1))
    return x * cos + swapped * sin


def _attn_lat_kernel(sink_ref, q_ref, k_ref, v_ref, ck_ref, cv_ref, cq_ref, sq_ref, cos_all_ref, sin_all_ref,
                     o_ref, kp_s, vp_s):
    i = pl.program_id(1)

    @pl.when(i == 0)
    def _():
        zeros = jnp.zeros((BLOCK, KV_W), BF16)
        kp_s[0:BLOCK, :] = zeros
        vp_s[0:BLOCK, :] = zeros
        kp_s[BLOCK + DEC_SEQ:, :] = zeros
        vp_s[BLOCK + DEC_SEQ:, :] = zeros
        kp_s[BLOCK:BLOCK + DEC_SEQ, :] = _rope(k_ref[...], cos_all_ref[...], sin_all_ref[...]).astype(BF16)
        vp_s[BLOCK:BLOCK + DEC_SEQ, :] = v_ref[...].astype(BF16)

    start = pl.multiple_of(i * BLOCK, BLOCK)
    kw = kp_s[pl.ds(start, 3 * BLOCK), :]
    vw = vp_s[pl.ds(start, 3 * BLOCK), :]
    ck = ck_ref[...].astype(BF16)
    cv = cv_ref[...].astype(BF16)
    qi = lax.broadcasted_iota(jnp.int32, (BLOCK, 3 * BLOCK), 0)
    ki = lax.broadcasted_iota(jnp.int32, (BLOCK, 3 * BLOCK), 1)
    rel = ki - BLOCK - qi
    kpos = ki + (start - BLOCK)
    ok = (jnp.abs(rel) <= WINDOW) & (kpos >= 0) & (kpos < DEC_SEQ)
    cq = cq_ref[...]
    sq = sq_ref[...]
    for pair in range(N_HEADS // 2):
        q2 = _rope(q_ref[:, pair * 128:(pair + 1) * 128], cq, sq).astype(BF16)
        for sub in range(2):
            hd = pair * 2 + sub
            kv = hd // GQA_G
            sk = sink_ref[hd]
            qh = q2[:, sub * HEAD_DIM:(sub + 1) * HEAD_DIM]
            sw = _dot_nt(qh, kw[:, kv * HEAD_DIM:(kv + 1) * HEAD_DIM]) * SCALE
            sw = jnp.where(ok, sw, NEG)
            sc = _dot_nt(qh, ck[:, kv * HEAD_DIM:(kv + 1) * HEAD_DIM]) * SCALE
            m = jnp.maximum(jnp.maximum(jnp.max(sw, axis=-1, keepdims=True),
                                        jnp.max(sc, axis=-1, keepdims=True)), sk)
            ew = jnp.exp(sw - m)
            ec = jnp.exp(sc - m)
            den = (jnp.sum(ew, axis=-1, keepdims=True) + jnp.sum(ec, axis=-1, keepdims=True)
                   + jnp.exp(sk - m))
            o = (_dot(ew.astype(BF16), vw[:, kv * HEAD_DIM:(kv + 1) * HEAD_DIM])
                 + _dot(ec.astype(BF16), cv[:, kv * HEAD_DIM:(kv + 1) * HEAD_DIM]))
            o_ref[:, hd * HEAD_DIM:(hd + 1) * HEAD_DIM] = o / den


def _attn_lat(p, sink, cache_k, cache_v):
    cos, sin = _rope_tables()
    nqb = DEC_SEQ // BLOCK
    q0 = N_CTX // BLOCK
    b0 = N_CTX // DEC_SEQ
    kcol = ATT_W // KV_W
    return pl.pallas_call(
        _attn_lat_kernel,
        grid=(DEC_BATCH, nqb),
        in_specs=[
            pl.BlockSpec(memory_space=pltpu.SMEM),
            pl.BlockSpec((BLOCK, ATT_W), lambda b, i: (q0 + b * nqb + i, 0)),
            pl.BlockSpec((DEC_SEQ, KV_W), lambda b, i: (b0 + b, kcol)),
            pl.BlockSpec((DEC_SEQ, KV_W), lambda b, i: (b0 + b, kcol + 1)),
            pl.BlockSpec((None, PAST_LEN, KV_W), lambda b, i: (b, 0, 0)),
            pl.BlockSpec((None, PAST_LEN, KV_W), lambda b, i: (b, 0, 0)),
            pl.BlockSpec((BLOCK, 128), lambda b, i: (i, 0)),
            pl.BlockSpec((BLOCK, 128), lambda b, i: (i, 0)),
            pl.BlockSpec((DEC_SEQ, 128), lambda b, i: (0, 0)),
            pl.BlockSpec((DEC_SEQ, 128), lambda b, i: (0, 0)),
        ],
        out_specs=pl.BlockSpec((BLOCK, ATT_W), lambda b, i: (b * nqb + i, 0)),
        out_shape=jax.ShapeDtypeStruct((N_LAT, ATT_W), F32),
        scratch_shapes=[pltpu.VMEM((DEC_SEQ + 2 * BLOCK, KV_W), BF16),
                        pltpu.VMEM((DEC_SEQ + 2 * BLOCK, KV_W), BF16)],
        compiler_params=_params("arbitrary", "arbitrary"),
        name="attn_lat",
    )(sink, p, p, p, cache_k, cache_v, cos, sin, cos, sin)


def _shift_rows(x, row, d):
    if d == 0:
        return x
    n = x.shape[0]
    y = pltpu.roll(x, d % n, 0)
    return jnp.where((row >= d) & (row < n + d), y, 0.0)


def _sigmoid(x):
    return 0.5 * jnp.tanh(0.5 * x) + 0.5


def _sqrt_nonneg(x):
    return jnp.where(x > 0.0, x * lax.rsqrt(x), 0.0)


def _gelu_tanh(x):
    return x * (0.5 * (1.0 + jnp.tanh(math.sqrt(2.0 / math.pi) * (x + 0.044715 * (x * x * x)))))


def _lru_kernel(xr_ref, g_ref, h0_ref, cw_ref, cb_ref, wg_ref, bg_ref, lam_ref, o_ref, st_ref,
                af_s, uf_s, ab_s, ub_s, hf_s, hb_s, *, L, G):
    sp = jnp.maximum(-lam_ref[...], 0.0) + jnp.log1p(jnp.exp(-jnp.abs(lam_ref[...])))
    nsp = -LRU_C * sp
    row = lax.broadcasted_iota(jnp.int32, (L, LCB), 0)
    left = LRU_CONV // 2

    def prep(b, carry):
        r0 = pl.multiple_of(b * L, L)
        x = xr_ref[pl.ds(r0, L), :]
        xc = cb_ref[...]
        for k in range(LRU_CONV):
            xc = xc + _shift_rows(x, row, left - k) * cw_ref[k:k + 1, :]
        gates = _dot(xc.astype(BF16), wg_ref[...]) + bg_ref[...]
        for d, (a_s, u_s) in enumerate(((af_s, uf_s), (ab_s, ub_s))):
            r = _sigmoid(gates[:, (2 * d) * LCB:(2 * d + 1) * LCB])
            gi = _sigmoid(gates[:, (2 * d + 1) * LCB:(2 * d + 2) * LCB])
            log_a = r * nsp[d:d + 1, :]
            a = jnp.exp(log_a)
            a_s[pl.ds(r0, L), :] = a
            u_s[pl.ds(r0, L), :] = _sqrt_nonneg(jnp.tanh(-log_a) * (1.0 + a * a)) * (gi * xc)
        return carry

    lax.fori_loop(0, G, prep, 0)

    def step(t, carry):
        hf, hb = carry
        tb = L - 1 - t
        hf = af_s[pl.ds(t, G, stride=L), :] * hf + uf_s[pl.ds(t, G, stride=L), :]
        hf_s[pl.ds(t, G, stride=L), :] = hf
        hb = ab_s[pl.ds(tb, G, stride=L), :] * hb + ub_s[pl.ds(tb, G, stride=L), :]
        hb_s[pl.ds(tb, G, stride=L), :] = hb
        return hf, hb

    hf, hb = lax.fori_loop(0, L, step, (h0_ref[0], h0_ref[1]), unroll=8)
    st_ref[0] = hf
    st_ref[1] = hb

    def fin(b, carry):
        r0 = pl.multiple_of(b * L, L)
        o_ref[pl.ds(r0, L), :] = (hf_s[pl.ds(r0, L), :] + hb_s[pl.ds(r0, L), :]) * _gelu_tanh(g_ref[pl.ds(r0, L), :])
        return carry

    lax.fori_loop(0, G, fin, 0)


def _lru(p, h0, conv_w, conv_b, wg, bg, lam, *, row0, L, B, G):
    nc = LRU_W // LCB
    xcol = (ATT_W + 2 * KV_W) // LCB
    gcol = xcol + nc
    rb0 = row0 // (G * L)
    kern = functools.partial(_lru_kernel, L=L, G=G)
    return pl.pallas_call(
        kern,
        grid=(B // G, nc),
        in_specs=[
            pl.BlockSpec((G * L, LCB), lambda i, c: (rb0 + i, xcol + c)),
            pl.BlockSpec((G * L, LCB), lambda i, c: (rb0 + i, gcol + c)),
            pl.BlockSpec((2, G, LCB), lambda i, c: (0, i, c)),
            pl.BlockSpec((LRU_CONV, LCB), lambda i, c: (0, c)),
            pl.BlockSpec((1, LCB), lambda i, c: (0, c)),
            pl.BlockSpec((None, LCB, 4 * LCB), lambda i, c: (c, 0, 0)),
            pl.BlockSpec((None, 1, 4 * LCB), lambda i, c: (c, 0, 0)),
            pl.BlockSpec((2, LCB), lambda i, c: (0, c)),
        ],
        out_specs=[
            pl.BlockSpec((G * L, LCB), lambda i, c: (i, c)),
            pl.BlockSpec((2, G, LCB), lambda i, c: (0, i, c)),
        ],
        out_shape=[jax.ShapeDtypeStruct((B * L, LRU_W), F32),
                   jax.ShapeDtypeStruct((2, B, LRU_W), F32)],
        scratch_shapes=[pltpu.VMEM((G * L, LCB), F32) for _ in range(6)],
        compiler_params=_params("arbitrary", "arbitrary"),
        name="lru",
    )(p, p, h0, conv_w, conv_b.reshape(1, LRU_W), wg, bg, lam)


LRU_TC = 64


def _lru_tm_kernel(xr_ref, g_ref, h0_ref, cw_ref, cb_ref, wg_ref, bg_ref, lam_ref, o_ref, st_ref,
                   xp_s, af_s, uf_s, ab_s, ub_s, hf_s, hb_s, *, L, G):
    sp = jnp.maximum(-lam_ref[...], 0.0) + jnp.log1p(jnp.exp(-jnp.abs(lam_ref[...])))
    nsp = -LRU_C * sp
    left = LRU_CONV // 2
    right = LRU_CONV - 1 - left
    xp_s[0:left] = jnp.zeros((left, G, LCB), F32)
    xp_s[left + L:left + L + right] = jnp.zeros((right, G, LCB), F32)
    xp_s[left:left + L] = jnp.swapaxes(xr_ref[...].reshape(G, L, LCB), 0, 1)

    def prep(c, carry):
        t0 = pl.multiple_of(c * LRU_TC, LRU_TC)
        xc = cb_ref[...].reshape(1, 1, LCB)
        for k in range(LRU_CONV):
            xc = xc + xp_s[pl.ds(t0 + k, LRU_TC)] * cw_ref[k:k + 1, :].reshape(1, 1, LCB)
        xc = xc.reshape(LRU_TC * G, LCB)
        gates = _dot(xc.astype(BF16), wg_ref[...]) + bg_ref[...]
        for d, (a_s, u_s) in enumerate(((af_s, uf_s), (ab_s, ub_s))):
            r = _sigmoid(gates[:, (2 * d) * LCB:(2 * d + 1) * LCB])
            gi = _sigmoid(gates[:, (2 * d + 1) * LCB:(2 * d + 2) * LCB])
            log_a = r * nsp[d:d + 1, :]
            a = jnp.exp(log_a)
            a_s[pl.ds(t0, LRU_TC)] = a.reshape(LRU_TC, G, LCB)
            u = _sqrt_nonneg(jnp.tanh(-log_a) * (1.0 + a * a)) * (gi * xc)
            u_s[pl.ds(t0, LRU_TC)] = u.reshape(LRU_TC, G, LCB)
        return carry

    lax.fori_loop(0, L // LRU_TC, prep, 0)

    def step(t, carry):
        hf, hb = carry
        tb = L - 1 - t
        hf = af_s[t] * hf + uf_s[t]
        hf_s[t] = hf
        hb = ab_s[tb] * hb + ub_s[tb]
        hb_s[tb] = hb
        return hf, hb

    hf, hb = lax.fori_loop(0, L, step, (h0_ref[0], h0_ref[1]), unroll=8)
    st_ref[0] = hf
    st_ref[1] = hb
    h = jnp.swapaxes(hf_s[...] + hb_s[...], 0, 1).reshape(G * L, LCB)
    o_ref[...] = h * _gelu_tanh(g_ref[...])


def _lru_tm(p, h0, conv_w, conv_b, wg, bg, lam, *, row0, L, B, G):
    nc = LRU_W // LCB
    xcol = (ATT_W + 2 * KV_W) // LCB
    gcol = xcol + nc
    rb0 = row0 // (G * L)
    tm = lambda extra: pltpu.VMEM((L + extra, G, LCB), F32)
    return pl.pallas_call(
        functools.partial(_lru_tm_kernel, L=L, G=G),
        grid=(B // G, nc),
        in_specs=[
            pl.BlockSpec((G * L, LCB), lambda i, c: (rb0 + i, xcol + c)),
            pl.BlockSpec((G * L, LCB), lambda i, c: (rb0 + i, gcol + c)),
            pl.BlockSpec((2, G, LCB), lambda i, c: (0, i, c)),
            pl.BlockSpec((LRU_CONV, LCB), lambda i, c: (0, c)),
            pl.BlockSpec((1, LCB), lambda i, c: (0, c)),
            pl.BlockSpec((None, LCB, 4 * LCB), lambda i, c: (c, 0, 0)),
            pl.BlockSpec((None, 1, 4 * LCB), lambda i, c: (c, 0, 0)),
            pl.BlockSpec((2, LCB), lambda i, c: (0, c)),
        ],
        out_specs=[
            pl.BlockSpec((G * L, LCB), lambda i, c: (i, c)),
            pl.BlockSpec((2, G, LCB), lambda i, c: (0, i, c)),
        ],
        out_shape=[jax.ShapeDtypeStruct((B * L, LRU_W), F32),
                   jax.ShapeDtypeStruct((2, B, LRU_W), F32)],
        scratch_shapes=[tm(LRU_CONV - 1)] + [tm(0) for _ in range(6)],
        compiler_params=_params("arbitrary", "arbitrary"),
        name="lru_tm",
    )(p, p, h0, conv_w, conv_b.reshape(1, LRU_W), wg, bg, lam)


def _lru_gate_weights(w_r, b_r, w_i, b_i):
    nc = LRU_W // LCB
    per = LCB // LRU_BD
    eye = jnp.eye(per, dtype=F32)

    def dense(w):
        w = w.reshape(nc, per, LRU_BD, LRU_BD)
        d = jnp.einsum('chij,hg->chigj', w, eye)
        return d.reshape(nc, LCB, LCB)

    cols = [dense(w_r[0]), dense(w_i[0]), dense(w_r[1]), dense(w_i[1])]
    wg = jnp.concatenate(cols, axis=-1).astype(BF16)
    bias = [b.reshape(nc, 1, LCB) for b in (b_r[0], b_i[0], b_r[1], b_i[1])]
    return wg, jnp.concatenate(bias, axis=-1)


def _fnet_kernel(f_ref, seq_ref, grp_ref, o_ref, cs_s, *, L, G):
    grp = grp_ref[...]
    for b in range(G):
        for gi in range(FNET_GROUPS):
            f = f_ref[b * L:(b + 1) * L, gi * FNET_GD:(gi + 1) * FNET_GD].astype(BF16)
            cs = _dot(f, grp).astype(BF16)
            col = b * FNET_W + gi * FNET_GD
            cs_s[0:L, col:col + FNET_GD] = cs[:, :FNET_GD]
            cs_s[L:2 * L, col:col + FNET_GD] = cs[:, FNET_GD:]
    y = _dot(seq_ref[...], cs_s[...])
    for b in range(G):
        o_ref[b * L:(b + 1) * L, :] = y[:, b * FNET_W:(b + 1) * FNET_W]


def _fnet(p, *, row0, L, B, G):
    seq, grp = _fnet_tables(L)
    rb0 = row0 // (G * L)
    return pl.pallas_call(
        functools.partial(_fnet_kernel, L=L, G=G),
        grid=(B // G,),
        in_specs=[
            pl.BlockSpec((G * L, FNET_W), lambda b: (rb0 + b, 0)),
            _resident((L, 2 * L), lambda b: (0, 0)),
            _resident((FNET_GD, 2 * FNET_GD), lambda b: (0, 0)),
        ],
        out_specs=pl.BlockSpec((G * L, FNET_W), lambda b: (b, 0)),
        out_shape=jax.ShapeDtypeStruct((B * L, FNET_W), F32),
        scratch_shapes=[pltpu.VMEM((2 * L, G * FNET_W), BF16)],
        compiler_params=_params("arbitrary"),
        name="fnet",
    )(p, _table_bf16(seq), _table_bf16(grp))


def _alternating_sign(row):
    return (1 - 2 * (row & 1)).astype(F32)


def _filter_kernel(zh_ref, zl_ref, w1_ref, b1_ref, w2_ref, b2_ref, w3f_ref, w3b_ref, fr_ref, ldf_ref, ldb_ref,
                   fc_ref, fs_ref, fcb_ref, fsb_ref, k1_ref, k2_ref, kn_ref, hid_hi_s, hid_lo_s, *, L):
    @pl.when((pl.program_id(0) == 0) & (pl.program_id(1) == 0))
    def _():
        hid = jnp.sin(fr_ref[0:1, :] * (_dot3((zh_ref[...], zl_ref[...]), _split(w1_ref[...])) + b1_ref[...]))
        hid = jnp.sin(fr_ref[1:2, :] * (_dot3(_split(hid), _split(w2_ref[...])) + b2_ref[...]))
        hid_hi_s[...], hid_lo_s[...] = _split(hid)

    hid = (hid_hi_s[...], hid_lo_s[...])
    tn = lax.broadcasted_iota(jnp.int32, (L, CB), 0).astype(F32) * (1.0 / L)
    ff = _dot3(hid, _split(w3f_ref[...])) * jnp.exp(-tn * jnp.exp(ldf_ref[...]))
    fb = _dot3(hid, _split(w3b_ref[...])) * jnp.exp(-tn * jnp.exp(ldb_ref[...]))
    ss = jnp.sum(ff * ff, axis=0, keepdims=True) + jnp.sum(fb * fb, axis=0, keepdims=True)
    scale = lax.rsqrt(ss + EPS)
    row = lax.broadcasted_iota(jnp.int32, (L, CB), 0)
    sgn = _alternating_sign(row)
    nyq = jnp.sum(sgn * ff, axis=0, keepdims=True) - jnp.sum(sgn * fb, axis=0, keepdims=True)
    ff = ff.astype(BF16)
    fb = fb.astype(BF16)
    kr = _dot(fc_ref[...], ff) + _dot(fcb_ref[...], fb)
    ki = _dot(fs_ref[...], ff) + _dot(fsb_ref[...], fb)
    w = jnp.where(row == 0, 0.5 / L, 1.0 / L) * scale
    k1_ref[0] = kr * w
    k2_ref[0] = ki * w
    kn_ref[0] = nyq * (scale * (0.5 / L))


def _hyena_filters(L, w1, b1, w2, b2, w3, freq, log_decay):
    zh, zl = _hyena_embedding(L)
    tabs = [_table_bf16(t) for t in _rdft_tables(L)]
    nc = HY_W // CB
    w1p = jnp.zeros((EMB_PAD, HY_HID), F32).at[:HY_EMB].set(w1)
    bcol = HY_ORDER * nc
    tab_spec = _resident((L, L), lambda n, c: (0, 0))
    small = lambda shape: _resident(shape, lambda n, c: (0, 0))
    return pl.pallas_call(
        functools.partial(_filter_kernel, L=L),
        grid=(HY_ORDER, nc),
        in_specs=[
            small((L, EMB_PAD)), small((L, EMB_PAD)),
            small((EMB_PAD, HY_HID)), small((1, HY_HID)),
            small((HY_HID, HY_HID)), small((1, HY_HID)),
            pl.BlockSpec((HY_HID, CB), lambda n, c: (0, n * nc + c)),
            pl.BlockSpec((HY_HID, CB), lambda n, c: (0, bcol + n * nc + c)),
            small((2, HY_HID)),
            pl.BlockSpec((1, CB), lambda n, c: (0, n * nc + c)),
            pl.BlockSpec((1, CB), lambda n, c: (0, bcol + n * nc + c)),
        ] + [tab_spec] * 4,
        out_specs=[
            pl.BlockSpec((1, L, CB), lambda n, c: (n, 0, c)),
            pl.BlockSpec((1, L, CB), lambda n, c: (n, 0, c)),
            pl.BlockSpec((1, 1, CB), lambda n, c: (n, 0, c)),
        ],
        out_shape=[jax.ShapeDtypeStruct((HY_ORDER, L, HY_W), F32),
                   jax.ShapeDtypeStruct((HY_ORDER, L, HY_W), F32),
                   jax.ShapeDtypeStruct((HY_ORDER, 1, HY_W), F32)],
        scratch_shapes=[pltpu.VMEM((L, HY_HID), BF16), pltpu.VMEM((L, HY_HID), BF16)],
        compiler_params=_params("arbitrary", "arbitrary"),
        name="hyena_filter",
    )(zh, zl, w1p, b1.reshape(1, HY_HID), w2, b2.reshape(1, HY_HID), w3, w3, freq,
      log_decay.reshape(1, HY_FILT), log_decay.reshape(1, HY_FILT), *tabs)


def _hyena_kernel(v_ref, x1_ref, x2_ref, cwv_ref, cw1_ref, cw2_ref, cbv_ref, cb1_ref, cb2_ref, bias_ref,
                  k1_ref, k2_ref, kn_ref, fc_ref, fs_ref, o_ref, *, L, G):
    row = lax.broadcasted_iota(jnp.int32, (L, CB), 0)
    left = HY_CONV // 2

    def conv(x_ref, cw_ref, cb_ref):
        outs = []
        for g in range(G):
            x = x_ref[g * L:(g + 1) * L, :]
            y = cb_ref[...]
            for k in range(HY_CONV):
                y = y + _shift_rows(x, row, left - k) * cw_ref[k:k + 1, :]
            outs.append(y)
        return outs[0] if G == 1 else jnp.concatenate(outs, axis=1)

    def per_seq(x):
        return x if G == 1 else jnp.concatenate([x] * G, axis=1)

    fc = fc_ref[...]
    fs = fs_ref[...]
    sgn = per_seq(_alternating_sign(row))
    z = conv(v_ref, cwv_ref, cbv_ref)
    gates = (conv(x1_ref, cw1_ref, cb1_ref), conv(x2_ref, cw2_ref, cb2_ref))
    for n in range(HY_ORDER):
        zb = z.astype(BF16)
        a = _dot(fc, zb)
        b = _dot(fs, zb)
        nyq = jnp.sum(sgn * z, axis=0, keepdims=True)
        k1 = per_seq(k1_ref[n])
        k2 = per_seq(k2_ref[n])
        p = a * k1 - b * k2
        q = a * k2 + b * k1
        y = _dot(fc, p.astype(BF16)) + _dot(fs, q.astype(BF16)) + sgn * (nyq * per_seq(kn_ref[n]))
        z = gates[n] * (y + per_seq(bias_ref[n:n + 1, :]) * z)
    for g in range(G):
        o_ref[g * L:(g + 1) * L, :] = z[:, g * CB:(g + 1) * CB]


def _hyena(p, conv_w, conv_b, hy_bias, k1, k2, kn, *, row0, L, B, G):
    fc, fs = (_table_bf16(t) for t in _rdft_tables(L)[:2])
    nc = HY_W // CB
    rb0 = row0 // (G * L)
    c0 = FNET_W // CB
    conv_b = conv_b.reshape(1, (HY_ORDER + 1) * HY_W)
    x_spec = lambda j: pl.BlockSpec((G * L, CB), lambda c, b: (rb0 + b, c0 + j * nc + c))
    cw_spec = lambda j: pl.BlockSpec((HY_CONV, CB), lambda c, b: (0, j * nc + c))
    cb_spec = lambda j: pl.BlockSpec((1, CB), lambda c, b: (0, j * nc + c))
    k_spec = pl.BlockSpec((HY_ORDER, L, CB), lambda c, b: (0, 0, c))
    return pl.pallas_call(
        functools.partial(_hyena_kernel, L=L, G=G),
        grid=(nc, B // G),
        in_specs=[x_spec(0), x_spec(1), x_spec(2), cw_spec(0), cw_spec(1), cw_spec(2),
                  cb_spec(0), cb_spec(1), cb_spec(2),
                  pl.BlockSpec((HY_ORDER, CB), lambda c, b: (0, c)),
                  k_spec, k_spec, pl.BlockSpec((HY_ORDER, 1, CB), lambda c, b: (0, 0, c))]
                 + [_resident((L, L), lambda c, b: (0, 0))] * 2,
        out_specs=pl.BlockSpec((G * L, CB), lambda c, b: (b, c)),
        out_shape=jax.ShapeDtypeStruct((B * L, HY_W), F32),
        compiler_params=_params("arbitrary", "arbitrary"),
        name="hyena",
    )(p, p, p, conv_w, conv_w, conv_w, conv_b, conv_b, conv_b, hy_bias, k1, k2, kn, fc, fs)


def kernel(x_prompt, x_sample, c, cache_k, cache_v, state_lru, c_ctx, mod_w, mod_b, norm_mix, norm_mlp, norm_final, mlp_w1, mlp_w2, ev_w_in, ev_w_out, attn_sink, lru_conv_w, lru_conv_b, lru_w_r, lru_b_r, lru_w_i, lru_b_i, lru_lambda, od_w_in, od_w_out, hy_conv_w, hy_conv_b, hy_w1, hy_b1, hy_w2, hy_b2, hy_w3, hy_freq, hy_log_decay, hy_bias):
    x = (x_prompt.reshape(N_CTX, D_MODEL), x_sample.reshape(N_LAT, D_MODEL))
    cvec = jnp.concatenate([c_ctx[None, :], c, jnp.zeros((8 - N_GROUPS, D_MODEL), F32)], axis=0)
    mods = _modulation(cvec, mod_w, mod_b)
    zero_state = jnp.zeros((2, BATCH, LRU_W), F32)
    ev_w_in_b, ev_w_out_b = ev_w_in.astype(BF16), ev_w_out.astype(BF16)
    od_w_in_b, od_w_out_b = od_w_in.astype(BF16), od_w_out.astype(BF16)
    w1_b, w2_b = mlp_w1[0].astype(BF16), mlp_w2[0].astype(BF16)
    k_list, v_list, s_list = [], [], []
    h_next = None
    for l in range(DEPTH):
        j = l // 2
        w_in = ev_w_in_b if l % 2 == 0 else od_w_in_b
        p = _pre(x, norm_mix[l], mods[l], w_in, j) if l == 0 else _project(h_next, w_in, j)
        if l % 2 == 0:
            att_ctx, k_ctx, v_ctx = _attn_ctx(p, attn_sink[j])
            k_list.append(k_ctx.reshape(BATCH, SEQ, N_KV, HEAD_DIM))
            v_list.append(v_ctx.reshape(BATCH, SEQ, N_KV, HEAD_DIM))
            att_lat = _attn_lat(p, attn_sink[j], cache_k[:, j].reshape(DEC_BATCH, PAST_LEN, KV_W),
                                cache_v[:, j].reshape(DEC_BATCH, PAST_LEN, KV_W))
            wg, bg = _lru_gate_weights(lru_w_r[j], lru_b_r[j], lru_w_i[j], lru_b_i[j])
            lru_args = (lru_conv_w[j], lru_conv_b[j], wg, bg, lru_lambda[j])
            y_ctx, st = _lru_tm(p, zero_state, *lru_args, row0=0, L=SEQ, B=BATCH, G=16)
            h0_lat = jnp.swapaxes(state_lru[:, j], 0, 1)
            y_lat, _ = _lru(p, h0_lat, *lru_args, row0=N_CTX, L=DEC_SEQ, B=DEC_BATCH, G=DEC_BATCH)
            s_list.append(jnp.swapaxes(st, 0, 1))
            m1, m2 = (att_ctx, att_lat), (y_ctx, y_lat)
            w_out = ev_w_out_b
        else:
            filt =(hy_w1[j], hy_b1[j], hy_w2[j], hy_b2[j], hy_w3[j], hy_freq[j], hy_log_decay[j])
            m1 = (_fnet(p, row0=0, L=SEQ, B=BATCH, G=8), _fnet(p, row0=N_CTX, L=DEC_SEQ, B=DEC_BATCH, G=2))
            hy = []
            for row0, L, B, G in ((0, SEQ, BATCH, 4), (N_CTX, DEC_SEQ, DEC_BATCH, 2)):
                k1, k2, kn = _hyena_filters(L, *filt)
                hy.append(_hyena(p, hy_conv_w[j], hy_conv_b[j], hy_bias[j], k1, k2, kn, row0=row0, L=L, B=B, G=G))
            m2 = tuple(hy)
            w_out = od_w_out_b
        last = l == DEPTH - 1
        post = functools.partial(_post, x, m1, m2, mods[l], mods[l if last else l + 1], norm_mlp[l],
                                 norm_final if last else norm_mix[l + 1], w_out, w1_b, w2_b, mlp_w1, mlp_w2, j, l)
        if not last:
            x, h_next, w1_b, w2_b = post(final=False)
        else:
            y_prompt = post(final=True, tile0=0, n_tiles=N_CTX_TILES).reshape(BATCH, SEQ, D_MODEL)
            y_sample = post(final=True, tile0=N_CTX_TILES, n_tiles=N_LAT // TM).reshape(DEC_BATCH, DEC_SEQ, D_MODEL)
    k_state = jnp.stack(k_list, axis=1)
    v_state = jnp.stack(v_list, axis=1)
    lru_state = jnp.stack(s_list, axis=1).astype(x_prompt.dtype)
    return (y_prompt, y_sample, k_state, v_state, lru_state)
```

```python
import functools
import math

import numpy as np
import jax
import jax.numpy as jnp
from jax import lax
from jax.experimental import pallas as pl
from jax.experimental.pallas import tpu as pltpu

F32 = jnp.float32
BF16 = jnp.bfloat16

D_MODEL = 1024
BATCH = 32
SEQ = 256
DEPTH = 4
DEC_BATCH = 2
DEC_SEQ = 1024
PAST_LEN = 256
GRID_W = 64
N_HEADS = 8
N_KV = 2
HEAD_DIM = 64
GQA_G = N_HEADS // N_KV
ATT_W = N_HEADS * HEAD_DIM
KV_W = N_KV * HEAD_DIM
WINDOW = 128
BLOCK = 128
ROPE_BASE = 10000.0
LRU_W = 512
LRU_BLOCKS = 8
LRU_BD = LRU_W // LRU_BLOCKS
LRU_CONV = 4
LRU_C = 8.0
FNET_GROUPS = 4
FNET_W = 512
FNET_GD = FNET_W // FNET_GROUPS
HY_W = 512
HY_ORDER = 2
HY_CONV = 3
HY_BANDS = 16
HY_EMB = 1 + 2 * HY_BANDS
HY_HID = 64
HY_FILT = 2 * HY_ORDER * HY_W
EVEN_IN = ATT_W + 2 * KV_W + 2 * LRU_W
ODD_IN = FNET_W + (HY_ORDER + 1) * HY_W
D_FF = 4 * D_MODEL
N_MOD = 6
EPS = 1e-6
NEG = -1e30

N_CTX = BATCH * SEQ
N_LAT = DEC_BATCH * DEC_SEQ
N_TOK = N_CTX + N_LAT
N_GROUPS = 1 + DEC_BATCH

VMEM_LIMIT = 56 * 1024 * 1024
TM = 512
CB = 256
LCB = 128
FF_CHUNK = 1024
EMB_PAD = 128


def _params(*sem):
    return pltpu.CompilerParams(dimension_semantics=sem, vmem_limit_bytes=VMEM_LIMIT)


def _resident(shape, index_map):
    return pl.BlockSpec(shape, index_map, pipeline_mode=pl.Buffered(1))


def _dot(a, b):
    return jnp.dot(a, b, preferred_element_type=F32)


def _dot_nt(a, b):
    return lax.dot_general(a, b, (((1,), (1,)), ((), ())), preferred_element_type=F32)


def _split(x):
    hi = x.astype(BF16)
    lo = (x - hi.astype(F32)).astype(BF16)
    return hi, lo


def _dot3(a, b):
    return _dot(a[0], b[0]) + (_dot(a[1], b[0]) + _dot(a[0], b[1]))


def _np_f32(x):
    return np.asarray(x, dtype=np.float32)


def _table_bf16(t):
    return jnp.asarray(t).astype(BF16)


def _np_split(x):
    hi = np.asarray(x, dtype=np.float32).astype(BF16)
    lo = (np.asarray(x, dtype=np.float64) - hi.astype(np.float64)).astype(np.float32).astype(BF16)
    return hi, lo


def _group_of_tile(i):
    return jnp.maximum(i * TM - (N_CTX - DEC_SEQ), 0) // DEC_SEQ


@functools.lru_cache(maxsize=None)
def _rdft_tables(L):
    k = np.arange(L, dtype=np.int64)[:, None]
    t = np.arange(L, dtype=np.int64)[None, :]
    ang = (k * t % (2 * L)).astype(np.float64) * (math.pi / L)
    tb = 2 * L - 1 - t
    angb = (k * tb % (2 * L)).astype(np.float64) * (math.pi / L)
    return tuple(_np_f32(a) for a in (np.cos(ang), -np.sin(ang), np.cos(angb), -np.sin(angb)))


@functools.lru_cache(maxsize=None)
def _fnet_tables(L):
    k = np.arange(L, dtype=np.int64)[:, None]
    t = np.arange(L, dtype=np.int64)[None, :]
    ang = (k * t % L).astype(np.float64) * (2.0 * math.pi / L)
    scale = 1.0 / math.sqrt(L * FNET_GD)
    seq = np.concatenate([np.cos(ang), -np.sin(ang)], axis=1) * scale
    m = np.arange(FNET_GD, dtype=np.int64)[:, None]
    d = np.arange(FNET_GD, dtype=np.int64)[None, :]
    angg = (m * d % FNET_GD).astype(np.float64) * (2.0 * math.pi / FNET_GD)
    grp = np.concatenate([np.cos(angg), np.sin(angg)], axis=1)
    return _np_f32(seq), _np_f32(grp)


@functools.lru_cache(maxsize=None)
def _rope_tables():
    L = DEC_SEQ
    n = HEAD_DIM // 4
    row = np.repeat(np.arange(L // GRID_W), GRID_W).astype(np.float32)
    col = np.tile(np.arange(GRID_W), L // GRID_W).astype(np.float32)
    inv = (np.float32(ROPE_BASE) ** (-np.arange(n, dtype=np.float32) / np.float32(n))).astype(np.float32)
    ar = (row[:, None] * inv[None, :]).astype(np.float64)
    ac = (col[:, None] * inv[None, :]).astype(np.float64)
    cos = np.concatenate([np.cos(ar), np.cos(ar), np.cos(ac), np.cos(ac)], axis=1)
    sin = np.concatenate([-np.sin(ar), np.sin(ar), -np.sin(ac), np.sin(ac)], axis=1)
    cos = np.tile(cos, (1, 2)).astype(np.float32)
    sin = np.tile(sin, (1, 2)).astype(np.float32)
    return cos, sin


@functools.lru_cache(maxsize=None)
def _hyena_embedding(L):
    t = np.arange(L, dtype=np.float32)
    tn = (t / np.float32(L)).astype(np.float64)
    bands = np.linspace(1e-4, HY_BANDS - 1, HY_BANDS, dtype=np.float32).astype(np.float64)
    w = (np.float32(2.0 * math.pi / L) * t).astype(np.float64)
    z = np.concatenate([tn[:, None], np.cos(w[:, None] * bands), -np.sin(w[:, None] * bands)], axis=-1)
    zp = np.zeros((L, EMB_PAD), np.float64)
    zp[:, :HY_EMB] = z
    return _np_split(zp)


MOD_TN = 1536


def _mod_kernel(c_ref, w_ref, b_ref, o_ref):
    c = c_ref[...]
    s_hi, s_lo = _split(c * jax.nn.sigmoid(c))
    r = _dot(jnp.concatenate([s_hi, s_lo], axis=0), w_ref[0].astype(BF16))
    o_ref[0] = r[:8] + r[8:] + b_ref[0]


def _modulation(cvec, mod_w, mod_b):
    out = pl.pallas_call(
        _mod_kernel,
        grid=(DEPTH, N_MOD * D_MODEL // MOD_TN),
        in_specs=[
            pl.BlockSpec((8, D_MODEL), lambda l, n: (0, 0)),
            pl.BlockSpec((1, D_MODEL, MOD_TN), lambda l, n: (l, 0, n)),
            pl.BlockSpec((1, 1, MOD_TN), lambda l, n: (l, 0, n)),
        ],
        out_specs=pl.BlockSpec((1, 8, MOD_TN), lambda l, n: (l, 0, n)),
        out_shape=jax.ShapeDtypeStruct((DEPTH, 8, N_MOD * D_MODEL), F32),
        compiler_params=_params("arbitrary", "arbitrary"),
        name="modulation",
    )(cvec, mod_w, mod_b.reshape(DEPTH, 1, N_MOD * D_MODEL))
    return out.reshape(DEPTH, 8, N_MOD, D_MODEL)[:, :N_GROUPS]


def _rms(x, g):
    return x * lax.rsqrt(jnp.mean(x * x, axis=-1, keepdims=True) + EPS) * g


N_CTX_TILES = N_CTX // TM


def _row_stream(x, tile0):
    if not isinstance(x, tuple):
        return [x], [pl.BlockSpec((TM, x.shape[1]), lambda i: (i + tile0, 0))]
    ctx, lat = x
    return [ctx, lat], [
        pl.BlockSpec((TM, ctx.shape[1]), lambda i: (jnp.minimum(i + tile0, N_CTX_TILES - 1), 0)),
        pl.BlockSpec((TM, lat.shape[1]), lambda i: (jnp.maximum(i + tile0 - N_CTX_TILES, 0), 0)),
    ]


def _read_rows(refs, paired, tile):
    vals = []
    k = 0
    for p in paired:
        if p:
            vals.append(jnp.where(tile < N_CTX_TILES, refs[k][...], refs[k + 1][...]))
            k += 2
        else:
            vals.append(refs[k][...])
            k += 1
    return vals, refs[k:]


def _pre_kernel(*refs, paired, tile0):
    (x,), (g_ref, mod_ref, w_ref, o_ref) = _read_rows(refs, paired, pl.program_id(0) + tile0)
    h = _rms(x, g_ref[...]) * (1.0 + mod_ref[1:2, :]) + mod_ref[0:1, :]
    o_ref[...] = _dot(h.astype(BF16), w_ref[...])


def _pre(x, g, mod, w_in, j):
    n_in = w_in.shape[2]
    ops, specs = _row_stream(x, 0)
    return pl.pallas_call(
        functools.partial(_pre_kernel, paired=(isinstance(x, tuple),), tile0=0),
        grid=(N_TOK // TM,),
        in_specs=specs + [
            _resident((1, D_MODEL), lambda i: (0, 0)),
            pl.BlockSpec((None, N_MOD, D_MODEL), lambda i: (_group_of_tile(i), 0, 0)),
            _resident((None, D_MODEL, n_in), lambda i: (j, 0, 0)),
        ],
        out_specs=pl.BlockSpec((TM, n_in), lambda i: (i, 0)),
        out_shape=jax.ShapeDtypeStruct((N_TOK, n_in), F32),
        compiler_params=_params("arbitrary"),
        name="pre",
    )(*ops, g.reshape(1, D_MODEL), mod, w_in)


def _post_kernel(*refs, paired, tile0, final):
    (x, m1, m2), rest = _read_rows(refs, paired, pl.program_id(0) + tile0)
    mod_ref, modn_ref, g_ref, gf_ref, wo_ref, w1_ref, w2_ref = rest[:7]
    if final:
        (o_ref,) = rest[7:]
    else:
        w1f_ref, w2f_ref, o_ref, hn_ref, w1n_ref, w2n_ref = rest[7:]
    half = wo_ref.shape[0] // 2
    o = _dot(m1.astype(BF16), wo_ref[:half, :]) + _dot(m2.astype(BF16), wo_ref[half:, :])
    x1 = x + mod_ref[2:3, :] * o
    h = (_rms(x1, g_ref[...]) * (1.0 + mod_ref[4:5, :]) + mod_ref[3:4, :]).astype(BF16)
    acc = jnp.zeros((TM, D_MODEL), F32)
    for c in range(D_FF // FF_CHUNK):
        t = jnp.maximum(_dot(h, w1_ref[:, c * FF_CHUNK:(c + 1) * FF_CHUNK]), 0.0)
        acc = acc + _dot((t * t).astype(BF16), w2_ref[c * FF_CHUNK:(c + 1) * FF_CHUNK, :])
    x2 = x1 + mod_ref[5:6, :] * acc
    if final:
        o_ref[...] = _rms(x2, gf_ref[...])
    else:
        o_ref[...] = x2
        hn_ref[...] = (_rms(x2, gf_ref[...]) * (1.0 + modn_ref[1:2, :]) + modn_ref[0:1, :]).astype(BF16)
        w1n_ref[...] = w1f_ref[...].astype(BF16)
        w2n_ref[...] = w2f_ref[...].astype(BF16)


W_CHUNKS = 16


def _post(x, m1, m2, mod, mod_next, g, g_tail, w_out, w1, w2, w1_f32, w2_f32, j, l, *, final, tile0=0,
          n_tiles=N_TOK // TM):
    mod_spec = pl.BlockSpec((None, N_MOD, D_MODEL), lambda i: (_group_of_tile(i + tile0), 0, 0))
    row_spec = pl.BlockSpec((TM, D_MODEL), lambda i: (i, 0))
    x_shape = jax.ShapeDtypeStruct((n_tiles * TM, D_MODEL), F32)
    extra_in, extra_ops, extra_out, extra_shapes = [], [], [], []
    if not final:
        assert n_tiles >= W_CHUNKS
        cw = D_FF // W_CHUNKS
        chunk = lambda i: jnp.minimum(i, W_CHUNKS - 1)
        extra_in = [pl.BlockSpec((None, D_MODEL, cw), lambda i: (l + 1, 0, chunk(i))),
                    pl.BlockSpec((None, cw, D_MODEL), lambda i: (l + 1, chunk(i), 0))]
        extra_ops = [w1_f32, w2_f32]
        extra_out = [row_spec, pl.BlockSpec((D_MODEL, cw), lambda i: (0, chunk(i))),
                     pl.BlockSpec((cw, D_MODEL), lambda i: (chunk(i), 0))]
        extra_shapes = [jax.ShapeDtypeStruct((n_tiles * TM, D_MODEL), BF16),
                        jax.ShapeDtypeStruct((D_MODEL, D_FF), BF16), jax.ShapeDtypeStruct((D_FF, D_MODEL), BF16)]
    ops, specs, paired = [], [], []
    for s in (x, m1, m2):
        o, sp = _row_stream(s, tile0)
        ops += o
        specs += sp
        paired.append(isinstance(s, tuple))
    return pl.pallas_call(
        functools.partial(_post_kernel, paired=tuple(paired), tile0=tile0, final=final),
        grid=(n_tiles,),
        in_specs=specs + [
            mod_spec,
            mod_spec,
            _resident((1, D_MODEL), lambda i: (0, 0)),
            _resident((1, D_MODEL), lambda i: (0, 0)),
            _resident((None,) + w_out.shape[1:], lambda i: (j, 0, 0)),
            _resident(w1.shape, lambda i: (0, 0)),
            _resident(w2.shape, lambda i: (0, 0)),
        ] + extra_in,
        out_specs=row_spec if final else [row_spec] + extra_out,
        out_shape=x_shape if final else [x_shape] + extra_shapes,
        compiler_params=_params("arbitrary"),
        name="post",
    )(*ops, mod, mod_next, g.reshape(1, D_MODEL), g_tail.reshape(1, D_MODEL), w_out, w1, w2, *extra_ops)


def _project_kernel(h_ref, w_ref, o_ref):
    o_ref[...] = _dot(h_ref[...], w_ref[...])


def _project(h, w_in, j):
    n_in = w_in.shape[2]
    return pl.pallas_call(
        _project_kernel,
        grid=(N_TOK // TM,),
        in_specs=[pl.BlockSpec((TM, D_MODEL), lambda i: (i, 0)),
                  _resident((None, D_MODEL, n_in), lambda i: (j, 0, 0))],
        out_specs=pl.BlockSpec((TM, n_in), lambda i: (i, 0)),
        out_shape=jax.ShapeDtypeStruct((N_TOK, n_in), F32),
        compiler_params=_params("arbitrary"),
        name="project",
    )(h, w_in)


SCALE = HEAD_DIM ** -0.5


def _attn_ctx_kernel(sink_ref, q_ref, k_ref, v_ref, *rest, first):
    o_ref, ko_ref, vo_ref = rest[-3:]
    if first:
        for slot in range(ko_ref.shape[0]):
            ko_ref[slot] = k_ref[...]
            vo_ref[slot] = v_ref[...]
    else:
        ko_ref[...] = k_ref[...]
        vo_ref[...] = v_ref[...]
    k = k_ref[...].astype(BF16)
    v = v_ref[...].astype(BF16)
    for hd in range(N_HEADS):
        kv = hd // GQA_G
        sk = sink_ref[hd]
        qh = q_ref[:, hd * HEAD_DIM:(hd + 1) * HEAD_DIM].astype(BF16)
        s = _dot_nt(qh, k[:, kv * HEAD_DIM:(kv + 1) * HEAD_DIM]) * SCALE
        m = jnp.maximum(jnp.max(s, axis=-1, keepdims=True), sk)
        e = jnp.exp(s - m)
        den = jnp.sum(e, axis=-1, keepdims=True) + jnp.exp(sk - m)
        o = _dot(e.astype(BF16), v[:, kv * HEAD_DIM:(kv + 1) * HEAD_DIM])
        o_ref[:, hd * HEAD_DIM:(hd + 1) * HEAD_DIM] = o / den


def _attn_ctx(p, sink, j, kv_state=None):
    kcol = ATT_W // KV_W
    n_even = (DEPTH + 1) // 2
    first = kv_state is None
    state_shape = jax.ShapeDtypeStruct((BATCH, n_even, SEQ, KV_W), F32)
    if first:
        state_spec = pl.BlockSpec((None, n_even, SEQ, KV_W), lambda b: (b, 0, 0, 0))
        extra_specs, extra_ops, aliases = [], [], {}
    else:
        state_spec = pl.BlockSpec((None, None, SEQ, KV_W), lambda b: (b, j, 0, 0))
        extra_specs = [pl.BlockSpec(memory_space=pl.ANY)] * 2
        extra_ops = list(kv_state)
        aliases = {4: 1, 5: 2}
    return pl.pallas_call(
        functools.partial(_attn_ctx_kernel, first=first),
        grid=(BATCH,),
        in_specs=[
            pl.BlockSpec(memory_space=pltpu.SMEM),
            pl.BlockSpec((SEQ, ATT_W), lambda b: (b, 0)),
            pl.BlockSpec((SEQ, KV_W), lambda b: (b, kcol)),
            pl.BlockSpec((SEQ, KV_W), lambda b: (b, kcol + 1)),
        ] + extra_specs,
        out_specs=[pl.BlockSpec((SEQ, ATT_W), lambda b: (b, 0)), state_spec, state_spec],
        out_shape=[jax.ShapeDtypeStruct((N_CTX, ATT_W), F32), state_shape, state_shape],
        input_output_aliases=aliases,
        compiler_params=_params("arbitrary"),
        name="attn_ctx",
    )(sink, p, p, p, *extra_ops)


def _rope(x, cos, sin):
    lane = lax.broadcasted_iota(jnp.int32, x.shape, 1)
    swapped = jnp.where(lane % 32 < 16, pltpu.roll(x, 128 - 16, 1), pltpu.roll(x, 16, 1))
    return x * cos + swapped * sin


def _attn_lat_kernel(sink_ref, q_ref, k_ref, v_ref, ck_ref, cv_ref, cq_ref, sq_ref, cos_all_ref, sin_all_ref,
                     o_ref, kp_s, vp_s):
    i = pl.program_id(1)

    @pl.when(i == 0)
    def _():
        zeros = jnp.zeros((BLOCK, KV_W), BF16)
        kp_s[0:BLOCK, :] = zeros
        vp_s[0:BLOCK, :] = zeros
        kp_s[BLOCK + DEC_SEQ:, :] = zeros
        vp_s[BLOCK + DEC_SEQ:, :] = zeros
        kp_s[BLOCK:BLOCK + DEC_SEQ, :] = _rope(k_ref[...], cos_all_ref[...], sin_all_ref[...]).astype(BF16)
        vp_s[BLOCK:BLOCK + DEC_SEQ, :] = v_ref[...].astype(BF16)

    start = pl.multiple_of(i * BLOCK, BLOCK)
    kw = kp_s[pl.ds(start, 3 * BLOCK), :]
    vw = vp_s[pl.ds(start, 3 * BLOCK), :]
    ck = ck_ref[...].astype(BF16)
    cv = cv_ref[...].astype(BF16)
    qi = lax.broadcasted_iota(jnp.int32, (BLOCK, 3 * BLOCK), 0)
    ki = lax.broadcasted_iota(jnp.int32, (BLOCK, 3 * BLOCK), 1)
    rel = ki - BLOCK - qi
    kpos = ki + (start - BLOCK)
    ok = (jnp.abs(rel) <= WINDOW) & (kpos >= 0) & (kpos < DEC_SEQ)
    cq = cq_ref[...]
    sq = sq_ref[...]
    for pair in range(N_HEADS // 2):
        q2 = _rope(q_ref[:, pair * 128:(pair + 1) * 128], cq, sq).astype(BF16)
        for sub in range(2):
            hd = pair * 2 + sub
            kv = hd // GQA_G
            sk = sink_ref[hd]
            qh = q2[:, sub * HEAD_DIM:(sub + 1) * HEAD_DIM]
            sw = _dot_nt(qh, kw[:, kv * HEAD_DIM:(kv + 1) * HEAD_DIM]) * SCALE
            sw = jnp.where(ok, sw, NEG)
            sc = _dot_nt(qh, ck[:, kv * HEAD_DIM:(kv + 1) * HEAD_DIM]) * SCALE
            m = jnp.maximum(jnp.maximum(jnp.max(sw, axis=-1, keepdims=True),
                                        jnp.max(sc, axis=-1, keepdims=True)), sk)
            ew = jnp.exp(sw - m)
            ec = jnp.exp(sc - m)
            den = (jnp.sum(ew, axis=-1, keepdims=True) + jnp.sum(ec, axis=-1, keepdims=True)
                   + jnp.exp(sk - m))
            o = (_dot(ew.astype(BF16), vw[:, kv * HEAD_DIM:(kv + 1) * HEAD_DIM])
                 + _dot(ec.astype(BF16), cv[:, kv * HEAD_DIM:(kv + 1) * HEAD_DIM]))
            o_ref[:, hd * HEAD_DIM:(hd + 1) * HEAD_DIM] = o / den


def _attn_lat(p, sink, cache_k, cache_v):
    cos, sin = _rope_tables()
    nqb = DEC_SEQ // BLOCK
    q0 = N_CTX // BLOCK
    b0 = N_CTX // DEC_SEQ
    kcol = ATT_W // KV_W
    return pl.pallas_call(
        _attn_lat_kernel,
        grid=(DEC_BATCH, nqb),
        in_specs=[
            pl.BlockSpec(memory_space=pltpu.SMEM),
            pl.BlockSpec((BLOCK, ATT_W), lambda b, i: (q0 + b * nqb + i, 0)),
            pl.BlockSpec((DEC_SEQ, KV_W), lambda b, i: (b0 + b, kcol)),
            pl.BlockSpec((DEC_SEQ, KV_W), lambda b, i: (b0 + b, kcol + 1)),
            pl.BlockSpec((None, PAST_LEN, KV_W), lambda b, i: (b, 0, 0)),
            pl.BlockSpec((None, PAST_LEN, KV_W), lambda b, i: (b, 0, 0)),
            pl.BlockSpec((BLOCK, 128), lambda b, i: (i, 0)),
            pl.BlockSpec((BLOCK, 128), lambda b, i: (i, 0)),
            pl.BlockSpec((DEC_SEQ, 128), lambda b, i: (0, 0)),
            pl.BlockSpec((DEC_SEQ, 128), lambda b, i: (0, 0)),
        ],
        out_specs=pl.BlockSpec((BLOCK, ATT_W), lambda b, i: (b * nqb + i, 0)),
        out_shape=jax.ShapeDtypeStruct((N_LAT, ATT_W), F32),
        scratch_shapes=[pltpu.VMEM((DEC_SEQ + 2 * BLOCK, KV_W), BF16),
                        pltpu.VMEM((DEC_SEQ + 2 * BLOCK, KV_W), BF16)],
        compiler_params=_params("arbitrary", "arbitrary"),
        name="attn_lat",
    )(sink, p, p, p, cache_k, cache_v, cos, sin, cos, sin)


def _shift_rows(x, row, d):
    if d == 0:
        return x
    n = x.shape[0]
    y = pltpu.roll(x, d % n, 0)
    return jnp.where((row >= d) & (row < n + d), y, 0.0)


def _sigmoid(x):
    return 0.5 * jnp.tanh(0.5 * x) + 0.5


def _sqrt_nonneg(x):
    return jnp.where(x > 0.0, x * lax.rsqrt(x), 0.0)


def _gelu_tanh(x):
    return x * (0.5 * (1.0 + jnp.tanh(math.sqrt(2.0 / math.pi) * (x + 0.044715 * (x * x * x)))))


def _lru_kernel(xr_ref, g_ref, h0_ref, cw_ref, cb_ref, wg_ref, bg_ref, lam_ref, o_ref, st_ref,
                af_s, uf_s, ab_s, ub_s, hf_s, hb_s, *, L, G):
    sp = jnp.maximum(-lam_ref[...], 0.0) + jnp.log1p(jnp.exp(-jnp.abs(lam_ref[...])))
    nsp = -LRU_C * sp
    row = lax.broadcasted_iota(jnp.int32, (L, LCB), 0)
    left = LRU_CONV // 2

    def prep(b, carry):
        r0 = pl.multiple_of(b * L, L)
        x = xr_ref[pl.ds(r0, L), :]
        xc = cb_ref[...]
        for k in range(LRU_CONV):
            xc = xc + _shift_rows(x, row, left - k) * cw_ref[k:k + 1, :]
        gates = _dot(xc.astype(BF16), wg_ref[...]) + bg_ref[...]
        for d, (a_s, u_s) in enumerate(((af_s, uf_s), (ab_s, ub_s))):
            r = _sigmoid(gates[:, (2 * d) * LCB:(2 * d + 1) * LCB])
            gi = _sigmoid(gates[:, (2 * d + 1) * LCB:(2 * d + 2) * LCB])
            log_a = r * nsp[d:d + 1, :]
            a = jnp.exp(log_a)
            a_s[pl.ds(r0, L), :] = a
            u_s[pl.ds(r0, L), :] = _sqrt_nonneg(jnp.tanh(-log_a) * (1.0 + a * a)) * (gi * xc)
        return carry

    lax.fori_loop(0, G, prep, 0)

    def step(t, carry):
        hf, hb = carry
        tb = L - 1 - t
        hf = af_s[pl.ds(t, G, stride=L), :] * hf + uf_s[pl.ds(t, G, stride=L), :]
        hf_s[pl.ds(t, G, stride=L), :] = hf
        hb = ab_s[pl.ds(tb, G, stride=L), :] * hb + ub_s[pl.ds(tb, G, stride=L), :]
        hb_s[pl.ds(tb, G, stride=L), :] = hb
        return hf, hb

    hf, hb = lax.fori_loop(0, L, step, (h0_ref[0], h0_ref[1]), unroll=8)
    st_ref[0] = hf
    st_ref[1] = hb

    def fin(b, carry):
        r0 = pl.multiple_of(b * L, L)
        o_ref[pl.ds(r0, L), :] = (hf_s[pl.ds(r0, L), :] + hb_s[pl.ds(r0, L), :]) * _gelu_tanh(g_ref[pl.ds(r0, L), :])
        return carry

    lax.fori_loop(0, G, fin, 0)


def _lru(p, h0, conv_w, conv_b, wg, bg, lam, *, row0, L, B, G):
    nc = LRU_W // LCB
    xcol = (ATT_W + 2 * KV_W) // LCB
    gcol = xcol + nc
    rb0 = row0 // (G * L)
    kern = functools.partial(_lru_kernel, L=L, G=G)
    return pl.pallas_call(
        kern,
        grid=(B // G, nc),
        in_specs=[
            pl.BlockSpec((G * L, LCB), lambda i, c: (rb0 + i, xcol + c)),
            pl.BlockSpec((G * L, LCB), lambda i, c: (rb0 + i, gcol + c)),
            pl.BlockSpec((2, G, LCB), lambda i, c: (0, i, c)),
            pl.BlockSpec((LRU_CONV, LCB), lambda i, c: (0, c)),
            pl.BlockSpec((1, LCB), lambda i, c: (0, c)),
            pl.BlockSpec((None, LCB, 4 * LCB), lambda i, c: (c, 0, 0)),
            pl.BlockSpec((None, 1, 4 * LCB), lambda i, c: (c, 0, 0)),
            pl.BlockSpec((2, LCB), lambda i, c: (0, c)),
        ],
        out_specs=[
            pl.BlockSpec((G * L, LCB), lambda i, c: (i, c)),
            pl.BlockSpec((2, G, LCB), lambda i, c: (0, i, c)),
        ],
        out_shape=[jax.ShapeDtypeStruct((B * L, LRU_W), F32),
                   jax.ShapeDtypeStruct((2, B, LRU_W), F32)],
        scratch_shapes=[pltpu.VMEM((G * L, LCB), F32) for _ in range(6)],
        compiler_params=_params("arbitrary", "arbitrary"),
        name="lru",
    )(p, p, h0, conv_w, conv_b.reshape(1, LRU_W), wg, bg, lam)


LRU_TC = 64


def _lru_tm_kernel(xr_ref, g_ref, h0_ref, cw_ref, cb_ref, wg_ref, bg_ref, lam_ref, o_ref, st_ref,
                   xp_s, af_s, uf_s, ab_s, ub_s, hf_s, hb_s, *, L, G):
    sp = jnp.maximum(-lam_ref[...], 0.0) + jnp.log1p(jnp.exp(-jnp.abs(lam_ref[...])))
    nsp = -LRU_C * sp
    left = LRU_CONV // 2
    right = LRU_CONV - 1 - left
    xp_s[0:left] = jnp.zeros((left, G, LCB), F32)
    xp_s[left + L:left + L + right] = jnp.zeros((right, G, LCB), F32)
    xp_s[left:left + L] = jnp.swapaxes(xr_ref[...].reshape(G, L, LCB), 0, 1)

    def prep(c, carry):
        t0 = pl.multiple_of(c * LRU_TC, LRU_TC)
        xc = cb_ref[...].reshape(1, 1, LCB)
        for k in range(LRU_CONV):
            xc = xc + xp_s[pl.ds(t0 + k, LRU_TC)] * cw_ref[k:k + 1, :].reshape(1, 1, LCB)
        xc = xc.reshape(LRU_TC * G, LCB)
        gates = _dot(xc.astype(BF16), wg_ref[...]) + bg_ref[...]
        for d, (a_s, u_s) in enumerate(((af_s, uf_s), (ab_s, ub_s))):
            r = _sigmoid(gates[:, (2 * d) * LCB:(2 * d + 1) * LCB])
            gi = _sigmoid(gates[:, (2 * d + 1) * LCB:(2 * d + 2) * LCB])
            log_a = r * nsp[d:d + 1, :]
            a = jnp.exp(log_a)
            a_s[pl.ds(t0, LRU_TC)] = a.reshape(LRU_TC, G, LCB)
            u = _sqrt_nonneg(jnp.tanh(-log_a) * (1.0 + a * a)) * (gi * xc)
            u_s[pl.ds(t0, LRU_TC)] = u.reshape(LRU_TC, G, LCB)
        return carry

    lax.fori_loop(0, L // LRU_TC, prep, 0)

    def step(t, carry):
        hf, hb = carry
        tb = L - 1 - t
        hf = af_s[t] * hf + uf_s[t]
        hf_s[t] = hf
        hb = ab_s[tb] * hb + ub_s[tb]
        hb_s[tb] = hb
        return hf, hb

    hf, hb = lax.fori_loop(0, L, step, (h0_ref[0], h0_ref[1]), unroll=8)
    st_ref[0] = hf
    st_ref[1] = hb
    h = jnp.swapaxes(hf_s[...] + hb_s[...], 0, 1).reshape(G * L, LCB)
    o_ref[...] = h * _gelu_tanh(g_ref[...])


def _lru_tm(p, h0, conv_w, conv_b, wg, bg, lam, *, row0, L, B, G):
    nc = LRU_W // LCB
    xcol = (ATT_W + 2 * KV_W) // LCB
    gcol = xcol + nc
    rb0 = row0 // (G * L)
    tm = lambda extra: pltpu.VMEM((L + extra, G, LCB), F32)
    return pl.pallas_call(
        functools.partial(_lru_tm_kernel, L=L, G=G),
        grid=(B // G, nc),
        in_specs=[
            pl.BlockSpec((G * L, LCB), lambda i, c: (rb0 + i, xcol + c)),
            pl.BlockSpec((G * L, LCB), lambda i, c: (rb0 + i, gcol + c)),
            pl.BlockSpec((2, G, LCB), lambda i, c: (0, i, c)),
            pl.BlockSpec((LRU_CONV, LCB), lambda i, c: (0, c)),
            pl.BlockSpec((1, LCB), lambda i, c: (0, c)),
            pl.BlockSpec((None, LCB, 4 * LCB), lambda i, c: (c, 0, 0)),
            pl.BlockSpec((None, 1, 4 * LCB), lambda i, c: (c, 0, 0)),
            pl.BlockSpec((2, LCB), lambda i, c: (0, c)),
        ],
        out_specs=[
            pl.BlockSpec((G * L, LCB), lambda i, c: (i, c)),
            pl.BlockSpec((2, G, LCB), lambda i, c: (0, i, c)),
        ],
        out_shape=[jax.ShapeDtypeStruct((B * L, LRU_W), F32),
                   jax.ShapeDtypeStruct((2, B, LRU_W), F32)],
        scratch_shapes=[tm(LRU_CONV - 1)] + [tm(0) for _ in range(6)],
        compiler_params=_params("arbitrary", "arbitrary"),
        name="lru_tm",
    )(p, p, h0, conv_w, conv_b.reshape(1, LRU_W), wg, bg, lam)


def _lru_gate_weights(w_r, b_r, w_i, b_i):
    nc = LRU_W // LCB
    per = LCB // LRU_BD
    eye = jnp.eye(per, dtype=F32)

    def dense(w):
        w = w.reshape(nc, per, LRU_BD, LRU_BD)
        d = jnp.einsum('chij,hg->chigj', w, eye)
        return d.reshape(nc, LCB, LCB)

    cols = [dense(w_r[0]), dense(w_i[0]), dense(w_r[1]), dense(w_i[1])]
    wg = jnp.concatenate(cols, axis=-1).astype(BF16)
    bias = [b.reshape(nc, 1, LCB) for b in (b_r[0], b_i[0], b_r[1], b_i[1])]
    return wg, jnp.concatenate(bias, axis=-1)


def _fnet_kernel(f_ref, seq_ref, grp_ref, o_ref, cs_s, *, L, G):
    grp = grp_ref[...]
    for b in range(G):
        for gi in range(FNET_GROUPS):
            f = f_ref[b * L:(b + 1) * L, gi * FNET_GD:(gi + 1) * FNET_GD].astype(BF16)
            cs = _dot(f, grp).astype(BF16)
            col = b * FNET_W + gi * FNET_GD
            cs_s[0:L, col:col + FNET_GD] = cs[:, :FNET_GD]
            cs_s[L:2 * L, col:col + FNET_GD] = cs[:, FNET_GD:]
    y = _dot(seq_ref[...], cs_s[...])
    for b in range(G):
        o_ref[b * L:(b + 1) * L, :] = y[:, b * FNET_W:(b + 1) * FNET_W]


def _fnet(p, *, row0, L, B, G):
    seq, grp = _fnet_tables(L)
    rb0 = row0 // (G * L)
    return pl.pallas_call(
        functools.partial(_fnet_kernel, L=L, G=G),
        grid=(B // G,),
        in_specs=[
            pl.BlockSpec((G * L, FNET_W), lambda b: (rb0 + b, 0)),
            _resident((L, 2 * L), lambda b: (0, 0)),
            _resident((FNET_GD, 2 * FNET_GD), lambda b: (0, 0)),
        ],
        out_specs=pl.BlockSpec((G * L, FNET_W), lambda b: (b, 0)),
        out_shape=jax.ShapeDtypeStruct((B * L, FNET_W), F32),
        scratch_shapes=[pltpu.VMEM((2 * L, G * FNET_W), BF16)],
        compiler_params=_params("arbitrary"),
        name="fnet",
    )(p, _table_bf16(seq), _table_bf16(grp))


def _alternating_sign(row):
    return (1 - 2 * (row & 1)).astype(F32)


def _filter_kernel(zh_ref, zl_ref, w1_ref, b1_ref, w2_ref, b2_ref, w3f_ref, w3b_ref, fr_ref, ldf_ref, ldb_ref,
                   fc_ref, fs_ref, fcb_ref, fsb_ref, k1_ref, k2_ref, kn_ref, hid_hi_s, hid_lo_s, *, L):
    @pl.when((pl.program_id(0) == 0) & (pl.program_id(1) == 0))
    def _():
        hid = jnp.sin(fr_ref[0:1, :] * (_dot3((zh_ref[...], zl_ref[...]), _split(w1_ref[...])) + b1_ref[...]))
        hid = jnp.sin(fr_ref[1:2, :] * (_dot3(_split(hid), _split(w2_ref[...])) + b2_ref[...]))
        hid_hi_s[...], hid_lo_s[...] = _split(hid)

    hid = (hid_hi_s[...], hid_lo_s[...])
    tn = lax.broadcasted_iota(jnp.int32, (L, CB), 0).astype(F32) * (1.0 / L)
    ff = _dot3(hid, _split(w3f_ref[...])) * jnp.exp(-tn * jnp.exp(ldf_ref[...]))
    fb = _dot3(hid, _split(w3b_ref[...])) * jnp.exp(-tn * jnp.exp(ldb_ref[...]))
    ss = jnp.sum(ff * ff, axis=0, keepdims=True) + jnp.sum(fb * fb, axis=0, keepdims=True)
    scale = lax.rsqrt(ss + EPS)
    row = lax.broadcasted_iota(jnp.int32, (L, CB), 0)
    sgn = _alternating_sign(row)
    nyq = jnp.sum(sgn * ff, axis=0, keepdims=True) - jnp.sum(sgn * fb, axis=0, keepdims=True)
    ff = ff.astype(BF16)
    fb = fb.astype(BF16)
    kr = _dot(fc_ref[...], ff) + _dot(fcb_ref[...], fb)
    ki = _dot(fs_ref[...], ff) + _dot(fsb_ref[...], fb)
    w = jnp.where(row == 0, 0.5 / L, 1.0 / L) * scale
    k1_ref[0] = kr * w
    k2_ref[0] = ki * w
    kn_ref[0] = nyq * (scale * (0.5 / L))


def _hyena_filters(L, w1, b1, w2, b2, w3, freq, log_decay):
    zh, zl = _hyena_embedding(L)
    tabs = [_table_bf16(t) for t in _rdft_tables(L)]
    nc = HY_W // CB
    w1p = jnp.zeros((EMB_PAD, HY_HID), F32).at[:HY_EMB].set(w1)
    bcol = HY_ORDER * nc
    tab_spec = _resident((L, L), lambda n, c: (0, 0))
    small = lambda shape: _resident(shape, lambda n, c: (0, 0))
    return pl.pallas_call(
        functools.partial(_filter_kernel, L=L),
        grid=(HY_ORDER, nc),
        in_specs=[
            small((L, EMB_PAD)), small((L, EMB_PAD)),
            small((EMB_PAD, HY_HID)), small((1, HY_HID)),
            small((HY_HID, HY_HID)), small((1, HY_HID)),
            pl.BlockSpec((HY_HID, CB), lambda n, c: (0, n * nc + c)),
            pl.BlockSpec((HY_HID, CB), lambda n, c: (0, bcol + n * nc + c)),
            small((2, HY_HID)),
            pl.BlockSpec((1, CB), lambda n, c: (0, n * nc + c)),
            pl.BlockSpec((1, CB), lambda n, c: (0, bcol + n * nc + c)),
        ] + [tab_spec] * 4,
        out_specs=[
            pl.BlockSpec((1, L, CB), lambda n, c: (n, 0, c)),
            pl.BlockSpec((1, L, CB), lambda n, c: (n, 0, c)),
            pl.BlockSpec((1, 1, CB), lambda n, c: (n, 0, c)),
        ],
        out_shape=[jax.ShapeDtypeStruct((HY_ORDER, L, HY_W), F32),
                   jax.ShapeDtypeStruct((HY_ORDER, L, HY_W), F32),
                   jax.ShapeDtypeStruct((HY_ORDER, 1, HY_W), F32)],
        scratch_shapes=[pltpu.VMEM((L, HY_HID), BF16), pltpu.VMEM((L, HY_HID), BF16)],
        compiler_params=_params("arbitrary", "arbitrary"),
        name="hyena_filter",
    )(zh, zl, w1p, b1.reshape(1, HY_HID), w2, b2.reshape(1, HY_HID), w3, w3, freq,
      log_decay.reshape(1, HY_FILT), log_decay.reshape(1, HY_FILT), *tabs)


def _hyena_kernel(v_ref, x1_ref, x2_ref, cwv_ref, cw1_ref, cw2_ref, cbv_ref, cb1_ref, cb2_ref, bias_ref,
                  k1_ref, k2_ref, kn_ref, fc_ref, fs_ref, o_ref, *, L, G):
    row = lax.broadcasted_iota(jnp.int32, (L, CB), 0)
    left = HY_CONV // 2

    def conv(x_ref, cw_ref, cb_ref):
        outs = []
        for g in range(G):
            x = x_ref[g * L:(g + 1) * L, :]
            y = cb_ref[...]
            for k in range(HY_CONV):
                y = y + _shift_rows(x, row, left - k) * cw_ref[k:k + 1, :]
            outs.append(y)
        return outs[0] if G == 1 else jnp.concatenate(outs, axis=1)

    def per_seq(x):
        return x if G == 1 else jnp.concatenate([x] * G, axis=1)

    fc = fc_ref[...]
    fs = fs_ref[...]
    sgn = per_seq(_alternating_sign(row))
    z = conv(v_ref, cwv_ref, cbv_ref)
    gates = (conv(x1_ref, cw1_ref, cb1_ref), conv(x2_ref, cw2_ref, cb2_ref))
    for n in range(HY_ORDER):
        zb = z.astype(BF16)
        a = _dot(fc, zb)
        b = _dot(fs, zb)
        nyq = jnp.sum(sgn * z, axis=0, keepdims=True)
        k1 = per_seq(k1_ref[n])
        k2 = per_seq(k2_ref[n])
        p = a * k1 - b * k2
        q = a * k2 + b * k1
        y = _dot(fc, p.astype(BF16)) + _dot(fs, q.astype(BF16)) + sgn * (nyq * per_seq(kn_ref[n]))
        z = gates[n] * (y + per_seq(bias_ref[n:n + 1, :]) * z)
    for g in range(G):
        o_ref[g * L:(g + 1) * L, :] = z[:, g * CB:(g + 1) * CB]


def _hyena(p, conv_w, conv_b, hy_bias, k1, k2, kn, *, row0, L, B, G):
    fc, fs = (_table_bf16(t) for t in _rdft_tables(L)[:2])
    nc = HY_W // CB
    rb0 = row0 // (G * L)
    c0 = FNET_W // CB
    conv_b = conv_b.reshape(1, (HY_ORDER + 1) * HY_W)
    x_spec = lambda j: pl.BlockSpec((G * L, CB), lambda c, b: (rb0 + b, c0 + j * nc + c))
    cw_spec = lambda j: pl.BlockSpec((HY_CONV, CB), lambda c, b: (0, j * nc + c))
    cb_spec = lambda j: pl.BlockSpec((1, CB), lambda c, b: (0, j * nc + c))
    k_spec = pl.BlockSpec((HY_ORDER, L, CB), lambda c, b: (0, 0, c))
    return pl.pallas_call(
        functools.partial(_hyena_kernel, L=L, G=G),
        grid=(nc, B // G),
        in_specs=[x_spec(0), x_spec(1), x_spec(2), cw_spec(0), cw_spec(1), cw_spec(2),
                  cb_spec(0), cb_spec(1), cb_spec(2),
                  pl.BlockSpec((HY_ORDER, CB), lambda c, b: (0, c)),
                  k_spec, k_spec, pl.BlockSpec((HY_ORDER, 1, CB), lambda c, b: (0, 0, c))]
                 + [_resident((L, L), lambda c, b: (0, 0))] * 2,
        out_specs=pl.BlockSpec((G * L, CB), lambda c, b: (b, c)),
        out_shape=jax.ShapeDtypeStruct((B * L, HY_W), F32),
        compiler_params=_params("arbitrary", "arbitrary"),
        name="hyena",
    )(p, p, p, conv_w, conv_w, conv_w, conv_b, conv_b, conv_b, hy_bias, k1, k2, kn, fc, fs)


def kernel(x_prompt, x_sample, c, cache_k, cache_v, state_lru, c_ctx, mod_w, mod_b, norm_mix, norm_mlp, norm_final, mlp_w1, mlp_w2, ev_w_in, ev_w_out, attn_sink, lru_conv_w, lru_conv_b, lru_w_r, lru_b_r, lru_w_i, lru_b_i, lru_lambda, od_w_in, od_w_out, hy_conv_w, hy_conv_b, hy_w1, hy_b1, hy_w2, hy_b2, hy_w3, hy_freq, hy_log_decay, hy_bias):
    x = (x_prompt.reshape(N_CTX, D_MODEL), x_sample.reshape(N_LAT, D_MODEL))
    cvec = jnp.concatenate([c_ctx[None, :], c, jnp.zeros((8 - N_GROUPS, D_MODEL), F32)], axis=0)
    mods = _modulation(cvec, mod_w, mod_b)
    zero_state = jnp.zeros((2, BATCH, LRU_W), F32)
    ev_w_in_b, ev_w_out_b = ev_w_in.astype(BF16), ev_w_out.astype(BF16)
    od_w_in_b, od_w_out_b = od_w_in.astype(BF16), od_w_out.astype(BF16)
    w1_b, w2_b = mlp_w1[0].astype(BF16), mlp_w2[0].astype(BF16)
    s_list = []
    h_next = kv_state = None
    for l in range(DEPTH):
        j = l // 2
        w_in = ev_w_in_b if l % 2 == 0 else od_w_in_b
        p = _pre(x, norm_mix[l], mods[l], w_in, j) if l == 0 else _project(h_next, w_in, j)
        if l % 2 == 0:
            att_ctx, *kv_state = _attn_ctx(p, attn_sink[j], j, kv_state)
            att_lat = _attn_lat(p, attn_sink[j], cache_k[:, j].reshape(DEC_BATCH, PAST_LEN, KV_W),
                                cache_v[:, j].reshape(DEC_BATCH, PAST_LEN, KV_W))
            wg, bg = _lru_gate_weights(lru_w_r[j], lru_b_r[j], lru_w_i[j], lru_b_i[j])
            lru_args = (lru_conv_w[j], lru_conv_b[j], wg, bg, lru_lambda[j])
            y_ctx, st = _lru_tm(p, zero_state, *lru_args, row0=0, L=SEQ, B=BATCH, G=16)
            h0_lat = jnp.swapaxes(state_lru[:, j], 0, 1)
            y_lat, _ = _lru(p, h0_lat, *lru_args, row0=N_CTX, L=DEC_SEQ, B=DEC_BATCH, G=DEC_BATCH)
            s_list.append(jnp.swapaxes(st, 0, 1))
            m1, m2 = (att_ctx, att_lat), (y_ctx, y_lat)
            w_out = ev_w_out_b
        else:
            filt =(hy_w1[j], hy_b1[j], hy_w2[j], hy_b2[j], hy_w3[j], hy_freq[j], hy_log_decay[j])
            m1 = (_fnet(p, row0=0, L=SEQ, B=BATCH, G=8), _fnet(p, row0=N_CTX, L=DEC_SEQ, B=DEC_BATCH, G=2))
            hy = []
            for row0, L, B, G in ((0, SEQ, BATCH, 4), (N_CTX, DEC_SEQ, DEC_BATCH, 2)):
                k1, k2, kn = _hyena_filters(L, *filt)
                hy.append(_hyena(p, hy_conv_w[j], hy_conv_b[j], hy_bias[j], k1, k2, kn, row0=row0, L=L, B=B, G=G))
            m2 = tuple(hy)
            w_out = od_w_out_b
        last = l == DEPTH - 1
        post = functools.partial(_post, x, m1, m2, mods[l], mods[l if last else l + 1], norm_mlp[l],
                                 norm_final if last else norm_mix[l + 1], w_out, w1_b, w2_b, mlp_w1, mlp_w2, j, l)
        if not last:
            x, h_next, w1_b, w2_b = post(final=False)
        else:
            y_prompt = post(final=True, tile0=0, n_tiles=N_CTX_TILES).reshape(BATCH, SEQ, D_MODEL)
            y_sample = post(final=True, tile0=N_CTX_TILES, n_tiles=N_LAT // TM).reshape(DEC_BATCH, DEC_SEQ, D_MODEL)
    k_state, v_state = (s.reshape(BATCH, -1, SEQ, N_KV, HEAD_DIM) for s in kv_state)
    lru_state = jnp.stack(s_list, axis=1).astype(x_prompt.dtype)
    return (y_prompt, y_sample, k_state, v_state, lru_state)
```

```python
import functools
import math

import numpy as np
import jax
import jax.numpy as jnp
from jax import lax
from jax.experimental import pallas as pl
from jax.experimental.pallas import tpu as pltpu

F32 = jnp.float32
BF16 = jnp.bfloat16

D_MODEL = 1024
BATCH = 32
SEQ = 256
DEPTH = 4
DEC_BATCH = 2
DEC_SEQ = 1024
PAST_LEN = 256
GRID_W = 64
N_HEADS = 8
N_KV = 2
HEAD_DIM = 64
GQA_G = N_HEADS // N_KV
ATT_W = N_HEADS * HEAD_DIM
KV_W = N_KV * HEAD_DIM
WINDOW = 128
BLOCK = 128
ROPE_BASE = 10000.0
LRU_W = 512
LRU_BLOCKS = 8
LRU_BD = LRU_W // LRU_BLOCKS
LRU_CONV = 4
LRU_C = 8.0
FNET_GROUPS = 4
FNET_W = 512
FNET_GD = FNET_W // FNET_GROUPS
HY_W = 512
HY_ORDER = 2
HY_CONV = 3
HY_BANDS = 16
HY_EMB = 1 + 2 * HY_BANDS
HY_HID = 64
HY_FILT = 2 * HY_ORDER * HY_W
EVEN_IN = ATT_W + 2 * KV_W + 2 * LRU_W
ODD_IN = FNET_W + (HY_ORDER + 1) * HY_W
D_FF = 4 * D_MODEL
N_MOD = 6
EPS = 1e-6
NEG = -1e30

N_CTX = BATCH * SEQ
N_LAT = DEC_BATCH * DEC_SEQ
N_TOK = N_CTX + N_LAT
N_GROUPS = 1 + DEC_BATCH

VMEM_LIMIT = 56 * 1024 * 1024
TM = 512
CB = 256
LCB = 128
FF_CHUNK = 1024
EMB_PAD = 128


def _params(*sem):
    return pltpu.CompilerParams(dimension_semantics=sem, vmem_limit_bytes=VMEM_LIMIT)


def _resident(shape, index_map):
    return pl.BlockSpec(shape, index_map, pipeline_mode=pl.Buffered(1))


def _dot(a, b):
    return jnp.dot(a, b, preferred_element_type=F32)


def _dot_nt(a, b):
    return lax.dot_general(a, b, (((1,), (1,)), ((), ())), preferred_element_type=F32)


def _split(x):
    hi = x.astype(BF16)
    lo = (x - hi.astype(F32)).astype(BF16)
    return hi, lo


def _dot3(a, b):
    return _dot(a[0], b[0]) + (_dot(a[1], b[0]) + _dot(a[0], b[1]))


def _np_f32(x):
    return np.asarray(x, dtype=np.float32)


def _table_bf16(t):
    return jnp.asarray(t).astype(BF16)


def _np_split(x):
    hi = np.asarray(x, dtype=np.float32).astype(BF16)
    lo = (np.asarray(x, dtype=np.float64) - hi.astype(np.float64)).astype(np.float32).astype(BF16)
    return hi, lo


def _group_of_tile(i):
    return jnp.maximum(i * TM - (N_CTX - DEC_SEQ), 0) // DEC_SEQ


@functools.lru_cache(maxsize=None)
def _rdft_tables(L):
    k = np.arange(L, dtype=np.int64)[:, None]
    t = np.arange(L, dtype=np.int64)[None, :]
    ang = (k * t % (2 * L)).astype(np.float64) * (math.pi / L)
    tb = 2 * L - 1 - t
    angb = (k * tb % (2 * L)).astype(np.float64) * (math.pi / L)
    return tuple(_np_f32(a) for a in (np.cos(ang), -np.sin(ang), np.cos(angb), -np.sin(angb)))


@functools.lru_cache(maxsize=None)
def _fnet_tables(L):
    k = np.arange(L, dtype=np.int64)[:, None]
    t = np.arange(L, dtype=np.int64)[None, :]
    ang = (k * t % L).astype(np.float64) * (2.0 * math.pi / L)
    scale = 1.0 / math.sqrt(L * FNET_GD)
    seq = np.concatenate([np.cos(ang), -np.sin(ang)], axis=1) * scale
    m = np.arange(FNET_GD, dtype=np.int64)[:, None]
    d = np.arange(FNET_GD, dtype=np.int64)[None, :]
    angg = (m * d % FNET_GD).astype(np.float64) * (2.0 * math.pi / FNET_GD)
    grp = np.concatenate([np.cos(angg), np.sin(angg)], axis=1)
    return _np_f32(seq), _np_f32(grp)


@functools.lru_cache(maxsize=None)
def _rope_tables():
    L = DEC_SEQ
    n = HEAD_DIM // 4
    row = np.repeat(np.arange(L // GRID_W), GRID_W).astype(np.float32)
    col = np.tile(np.arange(GRID_W), L // GRID_W).astype(np.float32)
    inv = (np.float32(ROPE_BASE) ** (-np.arange(n, dtype=np.float32) / np.float32(n))).astype(np.float32)
    ar = (row[:, None] * inv[None, :]).astype(np.float64)
    ac = (col[:, None] * inv[None, :]).astype(np.float64)
    cos = np.concatenate([np.cos(ar), np.cos(ar), np.cos(ac), np.cos(ac)], axis=1)
    sin = np.concatenate([-np.sin(ar), np.sin(ar), -np.sin(ac), np.sin(ac)], axis=1)
    cos = np.tile(cos, (1, 2)).astype(np.float32)
    sin = np.tile(sin, (1, 2)).astype(np.float32)
    return cos, sin


@functools.lru_cache(maxsize=None)
def _hyena_embedding(L):
    t = np.arange(L, dtype=np.float32)
    tn = (t / np.float32(L)).astype(np.float64)
    bands = np.linspace(1e-4, HY_BANDS - 1, HY_BANDS, dtype=np.float32).astype(np.float64)
    w = (np.float32(2.0 * math.pi / L) * t).astype(np.float64)
    z = np.concatenate([tn[:, None], np.cos(w[:, None] * bands), -np.sin(w[:, None] * bands)], axis=-1)
    zp = np.zeros((L, EMB_PAD), np.float64)
    zp[:, :HY_EMB] = z
    return _np_split(zp)


MOD_TN = 1536


def _mod_kernel(c_ref, w_ref, b_ref, o_ref):
    c = c_ref[...]
    s_hi, s_lo = _split(c * jax.nn.sigmoid(c))
    r = _dot(jnp.concatenate([s_hi, s_lo], axis=0), w_ref[0].astype(BF16))
    o_ref[0] = r[:8] + r[8:] + b_ref[0]


def _modulation(cvec, mod_w, mod_b):
    out = pl.pallas_call(
        _mod_kernel,
        grid=(DEPTH, N_MOD * D_MODEL // MOD_TN),
        in_specs=[
            pl.BlockSpec((8, D_MODEL), lambda l, n: (0, 0)),
            pl.BlockSpec((1, D_MODEL, MOD_TN), lambda l, n: (l, 0, n)),
            pl.BlockSpec((1, 1, MOD_TN), lambda l, n: (l, 0, n)),
        ],
        out_specs=pl.BlockSpec((1, 8, MOD_TN), lambda l, n: (l, 0, n)),
        out_shape=jax.ShapeDtypeStruct((DEPTH, 8, N_MOD * D_MODEL), F32),
        compiler_params=_params("arbitrary", "arbitrary"),
        name="modulation",
    )(cvec, mod_w, mod_b.reshape(DEPTH, 1, N_MOD * D_MODEL))
    return out.reshape(DEPTH, 8, N_MOD, D_MODEL)[:, :N_GROUPS]


def _rms(x, g):
    return x * lax.rsqrt(jnp.mean(x * x, axis=-1, keepdims=True) + EPS) * g


N_CTX_TILES = N_CTX // TM


def _row_stream(x, tile0):
    if not isinstance(x, tuple):
        return [x], [pl.BlockSpec((TM, x.shape[1]), lambda i: (i + tile0, 0))]
    ctx, lat = x
    return [ctx, lat], [
        pl.BlockSpec((TM, ctx.shape[1]), lambda i: (jnp.minimum(i + tile0, N_CTX_TILES - 1), 0)),
        pl.BlockSpec((TM, lat.shape[1]), lambda i: (jnp.maximum(i + tile0 - N_CTX_TILES, 0), 0)),
    ]


def _read_rows(refs, paired, tile):
    vals = []
    k = 0
    for p in paired:
        if p:
            vals.append(jnp.where(tile < N_CTX_TILES, refs[k][...], refs[k + 1][...]))
            k += 2
        else:
            vals.append(refs[k][...])
            k += 1
    return vals, refs[k:]


def _pre_kernel(*refs, paired, tile0):
    (x,), (g_ref, mod_ref, w_ref, o_ref) = _read_rows(refs, paired, pl.program_id(0) + tile0)
    h = _rms(x, g_ref[...]) * (1.0 + mod_ref[1:2, :]) + mod_ref[0:1, :]
    o_ref[...] = _dot(h.astype(BF16), w_ref[...])


def _pre(x, g, mod, w_in, j):
    n_in = w_in.shape[2]
    ops, specs = _row_stream(x, 0)
    return pl.pallas_call(
        functools.partial(_pre_kernel, paired=(isinstance(x, tuple),), tile0=0),
        grid=(N_TOK // TM,),
        in_specs=specs + [
            _resident((1, D_MODEL), lambda i: (0, 0)),
            pl.BlockSpec((None, N_MOD, D_MODEL), lambda i: (_group_of_tile(i), 0, 0)),
            _resident((None, D_MODEL, n_in), lambda i: (j, 0, 0)),
        ],
        out_specs=pl.BlockSpec((TM, n_in), lambda i: (i, 0)),
        out_shape=jax.ShapeDtypeStruct((N_TOK, n_in), F32),
        compiler_params=_params("arbitrary"),
        name="pre",
    )(*ops, g.reshape(1, D_MODEL), mod, w_in)


def _post_kernel(*refs, paired, tile0, final):
    (x, m1, m2), rest = _read_rows(refs, paired, pl.program_id(0) + tile0)
    mod_ref, modn_ref, g_ref, gf_ref, wo_ref, w1_ref, w2_ref = rest[:7]
    if final:
        (o_ref,) = rest[7:]
    else:
        w1f_ref, w2f_ref, o_ref, hn_ref, w1n_ref, w2n_ref = rest[7:]
    half = wo_ref.shape[0] // 2
    o = _dot(m1.astype(BF16), wo_ref[:half, :]) + _dot(m2.astype(BF16), wo_ref[half:, :])
    x1 = x + mod_ref[2:3, :] * o
    h = (_rms(x1, g_ref[...]) * (1.0 + mod_ref[4:5, :]) + mod_ref[3:4, :]).astype(BF16)
    acc = jnp.zeros((TM, D_MODEL), F32)
    for c in range(D_FF // FF_CHUNK):
        t = jnp.maximum(_dot(h, w1_ref[:, c * FF_CHUNK:(c + 1) * FF_CHUNK]), 0.0)
        acc = acc + _dot((t * t).astype(BF16), w2_ref[c * FF_CHUNK:(c + 1) * FF_CHUNK, :])
    x2 = x1 + mod_ref[5:6, :] * acc
    if final:
        o_ref[...] = _rms(x2, gf_ref[...])
    else:
        o_ref[...] = x2
        hn_ref[...] = (_rms(x2, gf_ref[...]) * (1.0 + modn_ref[1:2, :]) + modn_ref[0:1, :]).astype(BF16)
        w1n_ref[...] = w1f_ref[...].astype(BF16)
        w2n_ref[...] = w2f_ref[...].astype(BF16)


W_CHUNKS = 16


def _post(x, m1, m2, mod, mod_next, g, g_tail, w_out, w1, w2, w1_f32, w2_f32, j, l, *, final, tile0=0,
          n_tiles=N_TOK // TM):
    mod_spec = pl.BlockSpec((None, N_MOD, D_MODEL), lambda i: (_group_of_tile(i + tile0), 0, 0))
    row_spec = pl.BlockSpec((TM, D_MODEL), lambda i: (i, 0))
    x_shape = jax.ShapeDtypeStruct((n_tiles * TM, D_MODEL), F32)
    extra_in, extra_ops, extra_out, extra_shapes = [], [], [], []
    if not final:
        assert n_tiles >= W_CHUNKS
        cw = D_FF // W_CHUNKS
        chunk = lambda i: jnp.minimum(i, W_CHUNKS - 1)
        extra_in = [pl.BlockSpec((None, D_MODEL, cw), lambda i: (l + 1, 0, chunk(i))),
                    pl.BlockSpec((None, cw, D_MODEL), lambda i: (l + 1, chunk(i), 0))]
        extra_ops = [w1_f32, w2_f32]
        extra_out = [row_spec, pl.BlockSpec((D_MODEL, cw), lambda i: (0, chunk(i))),
                     pl.BlockSpec((cw, D_MODEL), lambda i: (chunk(i), 0))]
        extra_shapes = [jax.ShapeDtypeStruct((n_tiles * TM, D_MODEL), BF16),
                        jax.ShapeDtypeStruct((D_MODEL, D_FF), BF16), jax.ShapeDtypeStruct((D_FF, D_MODEL), BF16)]
    ops, specs, paired = [], [], []
    for s in (x, m1, m2):
        o, sp = _row_stream(s, tile0)
        ops += o
        specs += sp
        paired.append(isinstance(s, tuple))
    return pl.pallas_call(
        functools.partial(_post_kernel, paired=tuple(paired), tile0=tile0, final=final),
        grid=(n_tiles,),
        in_specs=specs + [
            mod_spec,
            mod_spec,
            _resident((1, D_MODEL), lambda i: (0, 0)),
            _resident((1, D_MODEL), lambda i: (0, 0)),
            _resident((None,) + w_out.shape[1:], lambda i: (j, 0, 0)),
            _resident(w1.shape, lambda i: (0, 0)),
            _resident(w2.shape, lambda i: (0, 0)),
        ] + extra_in,
        out_specs=row_spec if final else [row_spec] + extra_out,
        out_shape=x_shape if final else [x_shape] + extra_shapes,
        compiler_params=_params("arbitrary"),
        name="post",
    )(*ops, mod, mod_next, g.reshape(1, D_MODEL), g_tail.reshape(1, D_MODEL), w_out, w1, w2, *extra_ops)


def _project_kernel(h_ref, w_ref, o_ref):
    o_ref[...] = _dot(h_ref[...], w_ref[...])


def _project(h, w_in, j):
    n_in = w_in.shape[2]
    return pl.pallas_call(
        _project_kernel,
        grid=(N_TOK // TM,),
        in_specs=[pl.BlockSpec((TM, D_MODEL), lambda i: (i, 0)),
                  _resident((None, D_MODEL, n_in), lambda i: (j, 0, 0))],
        out_specs=pl.BlockSpec((TM, n_in), lambda i: (i, 0)),
        out_shape=jax.ShapeDtypeStruct((N_TOK, n_in), F32),
        compiler_params=_params("arbitrary"),
        name="project",
    )(h, w_in)


SCALE = HEAD_DIM ** -0.5


def _both_halves(x):
    lo = lax.broadcasted_iota(jnp.int32, x.shape, 1) < HEAD_DIM
    swapped = pltpu.roll(x, HEAD_DIM, 1)
    return jnp.where(lo, x, swapped).astype(BF16), jnp.where(lo, swapped, x).astype(BF16)


def _gqa_group(q_cols, k2, v2, sinks, mask):
    rows = q_cols[0].shape[0]
    lo = lax.broadcasted_iota(jnp.int32, (rows, 2 * HEAD_DIM), 1) < HEAD_DIM
    parts = []
    for qc in q_cols:
        parts += [jnp.where(lo, qc, 0.0), jnp.where(lo, 0.0, qc)]
    q = jnp.concatenate(parts, axis=0).astype(BF16)
    s = _dot_nt(q, k2) * SCALE
    if mask is not None:
        w = mask.shape[1]
        mask = jnp.concatenate([mask] * GQA_G, axis=0)
        s = jnp.concatenate([jnp.where(mask, s[:, :w], NEG), s[:, w:]], axis=1)
    sk = jnp.concatenate([jnp.full((rows, 1), sinks[g], F32) for g in range(GQA_G)], axis=0)
    m = jnp.maximum(jnp.max(s, axis=-1, keepdims=True), sk)
    e = jnp.exp(s - m)
    den = jnp.sum(e, axis=-1, keepdims=True) + jnp.exp(sk - m)
    o = _dot(e.astype(BF16), v2) / den
    return [jnp.where(lo, o[(2 * c) * rows:(2 * c + 1) * rows], o[(2 * c + 1) * rows:(2 * c + 2) * rows])
            for c in range(2)]


def _attn_ctx_kernel(sink_ref, q_ref, k_ref, v_ref, *rest, first):
    o_ref, ko_ref, vo_ref = rest[-3:]
    if first:
        for slot in range(ko_ref.shape[0]):
            ko_ref[slot] = k_ref[...]
            vo_ref[slot] = v_ref[...]
    else:
        ko_ref[...] = k_ref[...]
        vo_ref[...] = v_ref[...]
    k = k_ref[...].astype(BF16)
    v = v_ref[...].astype(BF16)
    for hd in range(N_HEADS):
        kv = hd // GQA_G
        sk = sink_ref[hd]
        qh = q_ref[:, hd * HEAD_DIM:(hd + 1) * HEAD_DIM].astype(BF16)
        s = _dot_nt(qh, k[:, kv * HEAD_DIM:(kv + 1) * HEAD_DIM]) * SCALE
        m = jnp.maximum(jnp.max(s, axis=-1, keepdims=True), sk)
        e = jnp.exp(s - m)
        den = jnp.sum(e, axis=-1, keepdims=True) + jnp.exp(sk - m)
        o = _dot(e.astype(BF16), v[:, kv * HEAD_DIM:(kv + 1) * HEAD_DIM])
        o_ref[:, hd * HEAD_DIM:(hd + 1) * HEAD_DIM] = o / den


def _attn_ctx(p, sink, j, kv_state=None):
    kcol = ATT_W // KV_W
    n_even = (DEPTH + 1) // 2
    first = kv_state is None
    state_shape = jax.ShapeDtypeStruct((BATCH, n_even, SEQ, KV_W), F32)
    if first:
        state_spec = pl.BlockSpec((None, n_even, SEQ, KV_W), lambda b: (b, 0, 0, 0))
        extra_specs, extra_ops, aliases = [], [], {}
    else:
        state_spec = pl.BlockSpec((None, None, SEQ, KV_W), lambda b: (b, j, 0, 0))
        extra_specs = [pl.BlockSpec(memory_space=pl.ANY)] * 2
        extra_ops = list(kv_state)
        aliases = {4: 1, 5: 2}
    return pl.pallas_call(
        functools.partial(_attn_ctx_kernel, first=first),
        grid=(BATCH,),
        in_specs=[
            pl.BlockSpec(memory_space=pltpu.SMEM),
            pl.BlockSpec((SEQ, ATT_W), lambda b: (b, 0)),
            pl.BlockSpec((SEQ, KV_W), lambda b: (b, kcol)),
            pl.BlockSpec((SEQ, KV_W), lambda b: (b, kcol + 1)),
        ] + extra_specs,
        out_specs=[pl.BlockSpec((SEQ, ATT_W), lambda b: (b, 0)), state_spec, state_spec],
        out_shape=[jax.ShapeDtypeStruct((N_CTX, ATT_W), F32), state_shape, state_shape],
        input_output_aliases=aliases,
        compiler_params=_params("arbitrary"),
        name="attn_ctx",
    )(sink, p, p, p, *extra_ops)


def _rope(x, cos, sin):
    lane = lax.broadcasted_iota(jnp.int32, x.shape, 1)
    swapped = jnp.where(lane % 32 < 16, pltpu.roll(x, 128 - 16, 1), pltpu.roll(x, 16, 1))
    return x * cos + swapped * sin


def _attn_lat_kernel(sink_ref, q_ref, k_ref, v_ref, ck_ref, cv_ref, cq_ref, sq_ref, cos_all_ref, sin_all_ref,
                     o_ref, kp_s, vp_s):
    i = pl.program_id(1)

    @pl.when(i == 0)
    def _():
        zeros = jnp.zeros((BLOCK, KV_W), BF16)
        k2 = _both_halves(_rope(k_ref[...], cos_all_ref[...], sin_all_ref[...]))
        v2 = _both_halves(v_ref[...])
        for kv in range(N_KV):
            for s_ref, val in ((kp_s, k2[kv]), (vp_s, v2[kv])):
                s_ref[kv, 0:BLOCK, :] = zeros
                s_ref[kv, BLOCK + DEC_SEQ:, :] = zeros
                s_ref[kv, BLOCK:BLOCK + DEC_SEQ, :] = val

    start = pl.multiple_of(i * BLOCK, BLOCK)
    ck2 = _both_halves(ck_ref[...])
    cv2 = _both_halves(cv_ref[...])
    qi = lax.broadcasted_iota(jnp.int32, (BLOCK, 3 * BLOCK), 0)
    ki = lax.broadcasted_iota(jnp.int32, (BLOCK, 3 * BLOCK), 1)
    rel = ki - BLOCK - qi
    kpos = ki + (start - BLOCK)
    ok = (jnp.abs(rel) <= WINDOW) & (kpos >= 0) & (kpos < DEC_SEQ)
    cq = cq_ref[...]
    sq = sq_ref[...]
    for kv in range(N_KV):
        k_all = jnp.concatenate([kp_s[kv, pl.ds(start, 3 * BLOCK), :], ck2[kv]], axis=0)
        v_all = jnp.concatenate([vp_s[kv, pl.ds(start, 3 * BLOCK), :], cv2[kv]], axis=0)
        cols = [_rope(q_ref[:, (2 * kv + c) * 128:(2 * kv + c + 1) * 128], cq, sq) for c in range(2)]
        sinks = [sink_ref[GQA_G * kv + g] for g in range(GQA_G)]
        outs = _gqa_group(cols, k_all, v_all, sinks, ok)
        for c in range(2):
            o_ref[:, (2 * kv + c) * 128:(2 * kv + c + 1) * 128] = outs[c]


def _attn_lat(p, sink, cache_k, cache_v):
    cos, sin = _rope_tables()
    nqb = DEC_SEQ // BLOCK
    q0 = N_CTX // BLOCK
    b0 = N_CTX // DEC_SEQ
    kcol = ATT_W // KV_W
    return pl.pallas_call(
        _attn_lat_kernel,
        grid=(DEC_BATCH, nqb),
        in_specs=[
            pl.BlockSpec(memory_space=pltpu.SMEM),
            pl.BlockSpec((BLOCK, ATT_W), lambda b, i: (q0 + b * nqb + i, 0)),
            pl.BlockSpec((DEC_SEQ, KV_W), lambda b, i: (b0 + b, kcol)),
            pl.BlockSpec((DEC_SEQ, KV_W), lambda b, i: (b0 + b, kcol + 1)),
            pl.BlockSpec((None, PAST_LEN, KV_W), lambda b, i: (b, 0, 0)),
            pl.BlockSpec((None, PAST_LEN, KV_W), lambda b, i: (b, 0, 0)),
            pl.BlockSpec((BLOCK, 128), lambda b, i: (i, 0)),
            pl.BlockSpec((BLOCK, 128), lambda b, i: (i, 0)),
            pl.BlockSpec((DEC_SEQ, 128), lambda b, i: (0, 0)),
            pl.BlockSpec((DEC_SEQ, 128), lambda b, i: (0, 0)),
        ],
        out_specs=pl.BlockSpec((BLOCK, ATT_W), lambda b, i: (b * nqb + i, 0)),
        out_shape=jax.ShapeDtypeStruct((N_LAT, ATT_W), F32),
        scratch_shapes=[pltpu.VMEM((N_KV, DEC_SEQ + 2 * BLOCK, KV_W), BF16),
                        pltpu.VMEM((N_KV, DEC_SEQ + 2 * BLOCK, KV_W), BF16)],
        compiler_params=_params("arbitrary", "arbitrary"),
        name="attn_lat",
    )(sink, p, p, p, cache_k, cache_v, cos, sin, cos, sin)


def _shift_rows(x, row, d):
    if d == 0:
        return x
    n = x.shape[0]
    y = pltpu.roll(x, d % n, 0)
    return jnp.where((row >= d) & (row < n + d), y, 0.0)


def _sigmoid(x):
    return 0.5 * jnp.tanh(0.5 * x) + 0.5


def _sqrt_nonneg(x):
    return jnp.where(x > 0.0, x * lax.rsqrt(x), 0.0)


def _gelu_tanh(x):
    return x * (0.5 * (1.0 + jnp.tanh(math.sqrt(2.0 / math.pi) * (x + 0.044715 * (x * x * x)))))


def _lru_kernel(xr_ref, g_ref, h0_ref, cw_ref, cb_ref, wg_ref, bg_ref, lam_ref, o_ref, st_ref,
                af_s, uf_s, ab_s, ub_s, hf_s, hb_s, *, L, G):
    sp = jnp.maximum(-lam_ref[...], 0.0) + jnp.log1p(jnp.exp(-jnp.abs(lam_ref[...])))
    nsp = -LRU_C * sp
    row = lax.broadcasted_iota(jnp.int32, (L, LCB), 0)
    left = LRU_CONV // 2

    def prep(b, carry):
        r0 = pl.multiple_of(b * L, L)
        x = xr_ref[pl.ds(r0, L), :]
        xc = cb_ref[...]
        for k in range(LRU_CONV):
            xc = xc + _shift_rows(x, row, left - k) * cw_ref[k:k + 1, :]
        gates = _dot(xc.astype(BF16), wg_ref[...]) + bg_ref[...]
        for d, (a_s, u_s) in enumerate(((af_s, uf_s), (ab_s, ub_s))):
            r = _sigmoid(gates[:, (2 * d) * LCB:(2 * d + 1) * LCB])
            gi = _sigmoid(gates[:, (2 * d + 1) * LCB:(2 * d + 2) * LCB])
            log_a = r * nsp[d:d + 1, :]
            a = jnp.exp(log_a)
            a_s[pl.ds(r0, L), :] = a
            u_s[pl.ds(r0, L), :] = _sqrt_nonneg(jnp.tanh(-log_a) * (1.0 + a * a)) * (gi * xc)
        return carry

    lax.fori_loop(0, G, prep, 0)

    def step(t, carry):
        hf, hb = carry
        tb = L - 1 - t
        hf = af_s[pl.ds(t, G, stride=L), :] * hf + uf_s[pl.ds(t, G, stride=L), :]
        hf_s[pl.ds(t, G, stride=L), :] = hf
        hb = ab_s[pl.ds(tb, G, stride=L), :] * hb + ub_s[pl.ds(tb, G, stride=L), :]
        hb_s[pl.ds(tb, G, stride=L), :] = hb
        return hf, hb

    hf, hb = lax.fori_loop(0, L, step, (h0_ref[0], h0_ref[1]), unroll=8)
    st_ref[0] = hf
    st_ref[1] = hb

    def fin(b, carry):
        r0 = pl.multiple_of(b * L, L)
        o_ref[pl.ds(r0, L), :] = (hf_s[pl.ds(r0, L), :] + hb_s[pl.ds(r0, L), :]) * _gelu_tanh(g_ref[pl.ds(r0, L), :])
        return carry

    lax.fori_loop(0, G, fin, 0)


def _lru(p, h0, conv_w, conv_b, wg, bg, lam, *, row0, L, B, G):
    nc = LRU_W // LCB
    xcol = (ATT_W + 2 * KV_W) // LCB
    gcol = xcol + nc
    rb0 = row0 // (G * L)
    kern = functools.partial(_lru_kernel, L=L, G=G)
    return pl.pallas_call(
        kern,
        grid=(B // G, nc),
        in_specs=[
            pl.BlockSpec((G * L, LCB), lambda i, c: (rb0 + i, xcol + c)),
            pl.BlockSpec((G * L, LCB), lambda i, c: (rb0 + i, gcol + c)),
            pl.BlockSpec((2, G, LCB), lambda i, c: (0, i, c)),
            pl.BlockSpec((LRU_CONV, LCB), lambda i, c: (0, c)),
            pl.BlockSpec((1, LCB), lambda i, c: (0, c)),
            pl.BlockSpec((None, LCB, 4 * LCB), lambda i, c: (c, 0, 0)),
            pl.BlockSpec((None, 1, 4 * LCB), lambda i, c: (c, 0, 0)),
            pl.BlockSpec((2, LCB), lambda i, c: (0, c)),
        ],
        out_specs=[
            pl.BlockSpec((G * L, LCB), lambda i, c: (i, c)),
            pl.BlockSpec((2, G, LCB), lambda i, c: (0, i, c)),
        ],
        out_shape=[jax.ShapeDtypeStruct((B * L, LRU_W), F32),
                   jax.ShapeDtypeStruct((2, B, LRU_W), F32)],
        scratch_shapes=[pltpu.VMEM((G * L, LCB), F32) for _ in range(6)],
        compiler_params=_params("arbitrary", "arbitrary"),
        name="lru",
    )(p, p, h0, conv_w, conv_b.reshape(1, LRU_W), wg, bg, lam)


LRU_TC = 64


def _lru_tm_kernel(xr_ref, g_ref, h0_ref, cw_ref, cb_ref, wg_ref, bg_ref, lam_ref, o_ref, st_ref,
                   xp_s, af_s, uf_s, ab_s, ub_s, hf_s, hb_s, *, L, G):
    sp = jnp.maximum(-lam_ref[...], 0.0) + jnp.log1p(jnp.exp(-jnp.abs(lam_ref[...])))
    nsp = -LRU_C * sp
    left = LRU_CONV // 2
    right = LRU_CONV - 1 - left
    xp_s[0:left] = jnp.zeros((left, G, LCB), F32)
    xp_s[left + L:left + L + right] = jnp.zeros((right, G, LCB), F32)
    xp_s[left:left + L] = jnp.swapaxes(xr_ref[...].reshape(G, L, LCB), 0, 1)

    def prep(c, carry):
        t0 = pl.multiple_of(c * LRU_TC, LRU_TC)
        xc = cb_ref[...].reshape(1, 1, LCB)
        for k in range(LRU_CONV):
            xc = xc + xp_s[pl.ds(t0 + k, LRU_TC)] * cw_ref[k:k + 1, :].reshape(1, 1, LCB)
        xc = xc.reshape(LRU_TC * G, LCB)
        gates = _dot(xc.astype(BF16), wg_ref[...]) + bg_ref[...]
        for d, (a_s, u_s) in enumerate(((af_s, uf_s), (ab_s, ub_s))):
            r = _sigmoid(gates[:, (2 * d) * LCB:(2 * d + 1) * LCB])
            gi = _sigmoid(gates[:, (2 * d + 1) * LCB:(2 * d + 2) * LCB])
            log_a = r * nsp[d:d + 1, :]
            a = jnp.exp(log_a)
            a_s[pl.ds(t0, LRU_TC)] = a.reshape(LRU_TC, G, LCB)
            u = _sqrt_nonneg(jnp.tanh(-log_a) * (1.0 + a * a)) * (gi * xc)
            u_s[pl.ds(t0, LRU_TC)] = u.reshape(LRU_TC, G, LCB)
        return carry

    lax.fori_loop(0, L // LRU_TC, prep, 0)

    def step(t, carry):
        hf, hb = carry
        tb = L - 1 - t
        hf = af_s[t] * hf + uf_s[t]
        hf_s[t] = hf
        hb = ab_s[tb] * hb + ub_s[tb]
        hb_s[tb] = hb
        return hf, hb

    hf, hb = lax.fori_loop(0, L, step, (h0_ref[0], h0_ref[1]), unroll=8)
    st_ref[0] = hf
    st_ref[1] = hb
    h = jnp.swapaxes(hf_s[...] + hb_s[...], 0, 1).reshape(G * L, LCB)
    o_ref[...] = h * _gelu_tanh(g_ref[...])


def _lru_tm(p, h0, conv_w, conv_b, wg, bg, lam, *, row0, L, B, G):
    nc = LRU_W // LCB
    xcol = (ATT_W + 2 * KV_W) // LCB
    gcol = xcol + nc
    rb0 = row0 // (G * L)
    tm = lambda extra: pltpu.VMEM((L + extra, G, LCB), F32)
    return pl.pallas_call(
        functools.partial(_lru_tm_kernel, L=L, G=G),
        grid=(B // G, nc),
        in_specs=[
            pl.BlockSpec((G * L, LCB), lambda i, c: (rb0 + i, xcol + c)),
            pl.BlockSpec((G * L, LCB), lambda i, c: (rb0 + i, gcol + c)),
            pl.BlockSpec((2, G, LCB), lambda i, c: (0, i, c)),
            pl.BlockSpec((LRU_CONV, LCB), lambda i, c: (0, c)),
            pl.BlockSpec((1, LCB), lambda i, c: (0, c)),
            pl.BlockSpec((None, LCB, 4 * LCB), lambda i, c: (c, 0, 0)),
            pl.BlockSpec((None, 1, 4 * LCB), lambda i, c: (c, 0, 0)),
            pl.BlockSpec((2, LCB), lambda i, c: (0, c)),
        ],
        out_specs=[
            pl.BlockSpec((G * L, LCB), lambda i, c: (i, c)),
            pl.BlockSpec((2, G, LCB), lambda i, c: (0, i, c)),
        ],
        out_shape=[jax.ShapeDtypeStruct((B * L, LRU_W), F32),
                   jax.ShapeDtypeStruct((2, B, LRU_W), F32)],
        scratch_shapes=[tm(LRU_CONV - 1)] + [tm(0) for _ in range(6)],
        compiler_params=_params("arbitrary", "arbitrary"),
        name="lru_tm",
    )(p, p, h0, conv_w, conv_b.reshape(1, LRU_W), wg, bg, lam)


def _lru_gate_weights(w_r, b_r, w_i, b_i):
    nc = LRU_W // LCB
    per = LCB // LRU_BD
    eye = jnp.eye(per, dtype=F32)

    def dense(w):
        w = w.reshape(nc, per, LRU_BD, LRU_BD)
        d = jnp.einsum('chij,hg->chigj', w, eye)
        return d.reshape(nc, LCB, LCB)

    cols = [dense(w_r[0]), dense(w_i[0]), dense(w_r[1]), dense(w_i[1])]
    wg = jnp.concatenate(cols, axis=-1).astype(BF16)
    bias = [b.reshape(nc, 1, LCB) for b in (b_r[0], b_i[0], b_r[1], b_i[1])]
    return wg, jnp.concatenate(bias, axis=-1)


def _fnet_kernel(f_ref, seq_ref, grp_ref, o_ref, cs_s, *, L, G):
    grp = grp_ref[...]
    for b in range(G):
        for gi in range(FNET_GROUPS):
            f = f_ref[b * L:(b + 1) * L, gi * FNET_GD:(gi + 1) * FNET_GD].astype(BF16)
            cs = _dot(f, grp).astype(BF16)
            col = b * FNET_W + gi * FNET_GD
            cs_s[0:L, col:col + FNET_GD] = cs[:, :FNET_GD]
            cs_s[L:2 * L, col:col + FNET_GD] = cs[:, FNET_GD:]
    y = _dot(seq_ref[...], cs_s[...])
    for b in range(G):
        o_ref[b * L:(b + 1) * L, :] = y[:, b * FNET_W:(b + 1) * FNET_W]


def _fnet(p, *, row0, L, B, G):
    seq, grp = _fnet_tables(L)
    rb0 = row0 // (G * L)
    return pl.pallas_call(
        functools.partial(_fnet_kernel, L=L, G=G),
        grid=(B // G,),
        in_specs=[
            pl.BlockSpec((G * L, FNET_W), lambda b: (rb0 + b, 0)),
            _resident((L, 2 * L), lambda b: (0, 0)),
            _resident((FNET_GD, 2 * FNET_GD), lambda b: (0, 0)),
        ],
        out_specs=pl.BlockSpec((G * L, FNET_W), lambda b: (b, 0)),
        out_shape=jax.ShapeDtypeStruct((B * L, FNET_W), F32),
        scratch_shapes=[pltpu.VMEM((2 * L, G * FNET_W), BF16)],
        compiler_params=_params("arbitrary"),
        name="fnet",
    )(p, _table_bf16(seq), _table_bf16(grp))


def _alternating_sign(row):
    return (1 - 2 * (row & 1)).astype(F32)


def _filter_kernel(zh_ref, zl_ref, w1_ref, b1_ref, w2_ref, b2_ref, w3f_ref, w3b_ref, fr_ref, ldf_ref, ldb_ref,
                   fc_ref, fs_ref, fcb_ref, fsb_ref, k1_ref, k2_ref, kn_ref, hid_hi_s, hid_lo_s, *, L):
    @pl.when((pl.program_id(0) == 0) & (pl.program_id(1) == 0))
    def _():
        hid = jnp.sin(fr_ref[0:1, :] * (_dot3((zh_ref[...], zl_ref[...]), _split(w1_ref[...])) + b1_ref[...]))
        hid = jnp.sin(fr_ref[1:2, :] * (_dot3(_split(hid), _split(w2_ref[...])) + b2_ref[...]))
        hid_hi_s[...], hid_lo_s[...] = _split(hid)

    hid = (hid_hi_s[...], hid_lo_s[...])
    tn = lax.broadcasted_iota(jnp.int32, (L, CB), 0).astype(F32) * (1.0 / L)
    ff = _dot3(hid, _split(w3f_ref[...])) * jnp.exp(-tn * jnp.exp(ldf_ref[...]))
    fb = _dot3(hid, _split(w3b_ref[...])) * jnp.exp(-tn * jnp.exp(ldb_ref[...]))
    ss = jnp.sum(ff * ff, axis=0, keepdims=True) + jnp.sum(fb * fb, axis=0, keepdims=True)
    scale = lax.rsqrt(ss + EPS)
    row = lax.broadcasted_iota(jnp.int32, (L, CB), 0)
    sgn = _alternating_sign(row)
    nyq = jnp.sum(sgn * ff, axis=0, keepdims=True) - jnp.sum(sgn * fb, axis=0, keepdims=True)
    ff = ff.astype(BF16)
    fb = fb.astype(BF16)
    kr = _dot(fc_ref[...], ff) + _dot(fcb_ref[...], fb)
    ki = _dot(fs_ref[...], ff) + _dot(fsb_ref[...], fb)
    w = jnp.where(row == 0, 0.5 / L, 1.0 / L) * scale
    k1_ref[0] = kr * w
    k2_ref[0] = ki * w
    kn_ref[0] = nyq * (scale * (0.5 / L))


def _hyena_filters(L, w1, b1, w2, b2, w3, freq, log_decay):
    zh, zl = _hyena_embedding(L)
    tabs = [_table_bf16(t) for t in _rdft_tables(L)]
    nc = HY_W // CB
    w1p = jnp.zeros((EMB_PAD, HY_HID), F32).at[:HY_EMB].set(w1)
    bcol = HY_ORDER * nc
    tab_spec = _resident((L, L), lambda n, c: (0, 0))
    small = lambda shape: _resident(shape, lambda n, c: (0, 0))
    return pl.pallas_call(
        functools.partial(_filter_kernel, L=L),
        grid=(HY_ORDER, nc),
        in_specs=[
            small((L, EMB_PAD)), small((L, EMB_PAD)),
            small((EMB_PAD, HY_HID)), small((1, HY_HID)),
            small((HY_HID, HY_HID)), small((1, HY_HID)),
            pl.BlockSpec((HY_HID, CB), lambda n, c: (0, n * nc + c)),
            pl.BlockSpec((HY_HID, CB), lambda n, c: (0, bcol + n * nc + c)),
            small((2, HY_HID)),
            pl.BlockSpec((1, CB), lambda n, c: (0, n * nc + c)),
            pl.BlockSpec((1, CB), lambda n, c: (0, bcol + n * nc + c)),
        ] + [tab_spec] * 4,
        out_specs=[
            pl.BlockSpec((1, L, CB), lambda n, c: (n, 0, c)),
            pl.BlockSpec((1, L, CB), lambda n, c: (n, 0, c)),
            pl.BlockSpec((1, 1, CB), lambda n, c: (n, 0, c)),
        ],
        out_shape=[jax.ShapeDtypeStruct((HY_ORDER, L, HY_W), F32),
                   jax.ShapeDtypeStruct((HY_ORDER, L, HY_W), F32),
                   jax.ShapeDtypeStruct((HY_ORDER, 1, HY_W), F32)],
        scratch_shapes=[pltpu.VMEM((L, HY_HID), BF16), pltpu.VMEM((L, HY_HID), BF16)],
        compiler_params=_params("arbitrary", "arbitrary"),
        name="hyena_filter",
    )(zh, zl, w1p, b1.reshape(1, HY_HID), w2, b2.reshape(1, HY_HID), w3, w3, freq,
      log_decay.reshape(1, HY_FILT), log_decay.reshape(1, HY_FILT), *tabs)


def _hyena_kernel(v_ref, x1_ref, x2_ref, cwv_ref, cw1_ref, cw2_ref, cbv_ref, cb1_ref, cb2_ref, bias_ref,
                  k1_ref, k2_ref, kn_ref, fc_ref, fs_ref, o_ref, *, L, G):
    row = lax.broadcasted_iota(jnp.int32, (L, CB), 0)
    left = HY_CONV // 2

    def conv(x_ref, cw_ref, cb_ref):
        outs = []
        for g in range(G):
            x = x_ref[g * L:(g + 1) * L, :]
            y = cb_ref[...]
            for k in range(HY_CONV):
                y = y + _shift_rows(x, row, left - k) * cw_ref[k:k + 1, :]
            outs.append(y)
        return outs[0] if G == 1 else jnp.concatenate(outs, axis=1)

    def per_seq(x):
        return x if G == 1 else jnp.concatenate([x] * G, axis=1)

    fc = fc_ref[...]
    fs = fs_ref[...]
    sgn = per_seq(_alternating_sign(row))
    z = conv(v_ref, cwv_ref, cbv_ref)
    gates = (conv(x1_ref, cw1_ref, cb1_ref), conv(x2_ref, cw2_ref, cb2_ref))
    for n in range(HY_ORDER):
        zb = z.astype(BF16)
        a = _dot(fc, zb)
        b = _dot(fs, zb)
        nyq = jnp.sum(sgn * z, axis=0, keepdims=True)
        k1 = per_seq(k1_ref[n])
        k2 = per_seq(k2_ref[n])
        p = a * k1 - b * k2
        q = a * k2 + b * k1
        y = _dot(fc, p.astype(BF16)) + _dot(fs, q.astype(BF16)) + sgn * (nyq * per_seq(kn_ref[n]))
        z = gates[n] * (y + per_seq(bias_ref[n:n + 1, :]) * z)
    for g in range(G):
        o_ref[g * L:(g + 1) * L, :] = z[:, g * CB:(g + 1) * CB]


def _hyena(p, conv_w, conv_b, hy_bias, k1, k2, kn, *, row0, L, B, G):
    fc, fs = (_table_bf16(t) for t in _rdft_tables(L)[:2])
    nc = HY_W // CB
    rb0 = row0 // (G * L)
    c0 = FNET_W // CB
    conv_b = conv_b.reshape(1, (HY_ORDER + 1) * HY_W)
    x_spec = lambda j: pl.BlockSpec((G * L, CB), lambda c, b: (rb0 + b, c0 + j * nc + c))
    cw_spec = lambda j: pl.BlockSpec((HY_CONV, CB), lambda c, b: (0, j * nc + c))
    cb_spec = lambda j: pl.BlockSpec((1, CB), lambda c, b: (0, j * nc + c))
    k_spec = pl.BlockSpec((HY_ORDER, L, CB), lambda c, b: (0, 0, c))
    return pl.pallas_call(
        functools.partial(_hyena_kernel, L=L, G=G),
        grid=(nc, B // G),
        in_specs=[x_spec(0), x_spec(1), x_spec(2), cw_spec(0), cw_spec(1), cw_spec(2),
                  cb_spec(0), cb_spec(1), cb_spec(2),
                  pl.BlockSpec((HY_ORDER, CB), lambda c, b: (0, c)),
                  k_spec, k_spec, pl.BlockSpec((HY_ORDER, 1, CB), lambda c, b: (0, 0, c))]
                 + [_resident((L, L), lambda c, b: (0, 0))] * 2,
        out_specs=pl.BlockSpec((G * L, CB), lambda c, b: (b, c)),
        out_shape=jax.ShapeDtypeStruct((B * L, HY_W), F32),
        compiler_params=_params("arbitrary", "arbitrary"),
        name="hyena",
    )(p, p, p, conv_w, conv_w, conv_w, conv_b, conv_b, conv_b, hy_bias, k1, k2, kn, fc, fs)


def kernel(x_prompt, x_sample, c, cache_k, cache_v, state_lru, c_ctx, mod_w, mod_b, norm_mix, norm_mlp, norm_final, mlp_w1, mlp_w2, ev_w_in, ev_w_out, attn_sink, lru_conv_w, lru_conv_b, lru_w_r, lru_b_r, lru_w_i, lru_b_i, lru_lambda, od_w_in, od_w_out, hy_conv_w, hy_conv_b, hy_w1, hy_b1, hy_w2, hy_b2, hy_w3, hy_freq, hy_log_decay, hy_bias):
    x = (x_prompt.reshape(N_CTX, D_MODEL), x_sample.reshape(N_LAT, D_MODEL))
    cvec = jnp.concatenate([c_ctx[None, :], c, jnp.zeros((8 - N_GROUPS, D_MODEL), F32)], axis=0)
    mods = _modulation(cvec, mod_w, mod_b)
    zero_state = jnp.zeros((2, BATCH, LRU_W), F32)
    ev_w_in_b, ev_w_out_b = ev_w_in.astype(BF16), ev_w_out.astype(BF16)
    od_w_in_b, od_w_out_b = od_w_in.astype(BF16), od_w_out.astype(BF16)
    w1_b, w2_b = mlp_w1[0].astype(BF16), mlp_w2[0].astype(BF16)
    s_list = []
    h_next = kv_state = None
    for l in range(DEPTH):
        j = l // 2
        w_in = ev_w_in_b if l % 2 == 0 else od_w_in_b
        p = _pre(x, norm_mix[l], mods[l], w_in, j) if l == 0 else _project(h_next, w_in, j)
        if l % 2 == 0:
            att_ctx, *kv_state = _attn_ctx(p, attn_sink[j], j, kv_state)
            att_lat = _attn_lat(p, attn_sink[j], cache_k[:, j].reshape(DEC_BATCH, PAST_LEN, KV_W),
                                cache_v[:, j].reshape(DEC_BATCH, PAST_LEN, KV_W))
            wg, bg = _lru_gate_weights(lru_w_r[j], lru_b_r[j], lru_w_i[j], lru_b_i[j])
            lru_args = (lru_conv_w[j], lru_conv_b[j], wg, bg, lru_lambda[j])
            y_ctx, st = _lru_tm(p, zero_state, *lru_args, row0=0, L=SEQ, B=BATCH, G=16)
            h0_lat = jnp.swapaxes(state_lru[:, j], 0, 1)
            y_lat, _ = _lru(p, h0_lat, *lru_args, row0=N_CTX, L=DEC_SEQ, B=DEC_BATCH, G=DEC_BATCH)
            s_list.append(jnp.swapaxes(st, 0, 1))
            m1, m2 = (att_ctx, att_lat), (y_ctx, y_lat)
            w_out = ev_w_out_b
        else:
            filt =(hy_w1[j], hy_b1[j], hy_w2[j], hy_b2[j], hy_w3[j], hy_freq[j], hy_log_decay[j])
            m1 = (_fnet(p, row0=0, L=SEQ, B=BATCH, G=8), _fnet(p, row0=N_CTX, L=DEC_SEQ, B=DEC_BATCH, G=2))
            hy = []
            for row0, L, B, G in ((0, SEQ, BATCH, 4), (N_CTX, DEC_SEQ, DEC_BATCH, 2)):
                k1, k2, kn = _hyena_filters(L, *filt)
                hy.append(_hyena(p, hy_conv_w[j], hy_conv_b[j], hy_bias[j], k1, k2, kn, row0=row0, L=L, B=B, G=G))
            m2 = tuple(hy)
            w_out = od_w_out_b
        last = l == DEPTH - 1
        post = functools.partial(_post, x, m1, m2, mods[l], mods[l if last else l + 1], norm_mlp[l],
                                 norm_final if last else norm_mix[l + 1], w_out, w1_b, w2_b, mlp_w1, mlp_w2, j, l)
        if not last:
            x, h_next, w1_b, w2_b = post(final=False)
        else:
            y_prompt = post(final=True, tile0=0, n_tiles=N_CTX_TILES).reshape(BATCH, SEQ, D_MODEL)
            y_sample = post(final=True, tile0=N_CTX_TILES, n_tiles=N_LAT // TM).reshape(DEC_BATCH, DEC_SEQ, D_MODEL)
    k_state, v_state = (s.reshape(BATCH, -1, SEQ, N_KV, HEAD_DIM) for s in kv_state)
    lru_state = jnp.stack(s_list, axis=1).astype(x_prompt.dtype)
    return (y_prompt, y_sample, k_state, v_state, lru_state)
```
